```python
import math
import jax, jax.numpy as jnp
from jax import lax
import numpy as np

D_MODEL = 2048
BATCH = 4
SEQ = 4096
DEPTH = 2
DEC_BATCH = 128
DEC_SEQ = 8
PAST_LEN = 16384
PAGE_SIZE = 128

MIX_W = D_MODEL
RWKV_D = MIX_W // 2
RWKV_HEAD = 64
RWKV_H = RWKV_D // RWKV_HEAD
W_LORA = 64
A_LORA = 64
V_LORA = 32
G_LORA = 160
RWKV_W = 3 * RWKV_D + W_LORA + A_LORA + G_LORA
RWKV_SPLITS = [RWKV_D, 2 * RWKV_D, 3 * RWKV_D, 3 * RWKV_D + W_LORA, 3 * RWKV_D + W_LORA + A_LORA]
MLA_D = MIX_W - RWKV_D
MLA_V = 128
MLA_H = MLA_D // MLA_V
NOPE = 128
ROPE = 64
Q_LORA = 512
KV_LORA = 256
KV_W = KV_LORA + ROPE
IN_W = RWKV_W + Q_LORA + KV_W
ROPE_THETA = 10000.0
Q_BLOCK = 128
D_FF = 5632
N_EXPERTS = 8
TOP_K = 2
D_EXP = 2816
MOE_BLOCK = 256
N_DENSE = (DEPTH + 1) // 2
N_MOE = DEPTH // 2
RMS_EPS = 1e-6
GN_EPS = 64e-5

kernel_name = 'hybrid_rwkv7_mla_moe_step'


def rms_norm(x, g):
    xf = x.astype(jnp.float32)
    y = xf * lax.rsqrt(jnp.mean(xf * xf, axis=-1, keepdims=True) + RMS_EPS)
    return (y * g.astype(jnp.float32)).astype(x.dtype)


def rotary(x, pos):
    half = ROPE // 2
    inv_freq = ROPE_THETA ** (-jnp.arange(half, dtype=jnp.float32) * (2.0 / ROPE))
    ang = pos.astype(jnp.float32)[:, None] * inv_freq[None, :]
    shape = (pos.shape[0],) + (1,) * (x.ndim - 3) + (half,)
    cos, sin = jnp.cos(ang).reshape(shape), jnp.sin(ang).reshape(shape)
    xf = x.astype(jnp.float32)
    x1, x2 = xf[..., :half], xf[..., half:]
    return jnp.concatenate([x1 * cos - x2 * sin, x1 * sin + x2 * cos], axis=-1).astype(x.dtype)


def swiglu(x, w_gate, w_up, w_down):
    return (jax.nn.silu(x @ w_gate) * (x @ w_up)) @ w_down


def wkv7_scan(r, decay, k, v, kk, a, s0):
    def step(s, inp):
        r_t, w_t, k_t, v_t, ai_t, b_t = inp
        s = (s * w_t[:, :, None, :]
             + jnp.einsum('bhvk,bhk->bhv', s, ai_t)[..., None] * b_t[:, :, None, :]
             + v_t[..., None] * k_t[:, :, None, :])
        return s, jnp.einsum('bhvk,bhk->bhv', s, r_t)
    tm = lambda z: jnp.moveaxis(z.astype(jnp.float32), 1, 0)
    xs = (tm(r), tm(decay), tm(k), tm(v), -tm(kk), tm(kk) * tm(a))
    s_last, ys = lax.scan(step, s0.astype(jnp.float32), xs)
    return jnp.moveaxis(ys, 0, 1), s_last


def rwkv7_time_mix(p, shift_prev, s0, v_first, mu, w0, w_decay, a0, w_iclr, w_gate_out,
                   k_k, k_a, r_k, lnx_w, lnx_b, vres):
    bsz, t = p.shape[:2]
    prev = jnp.concatenate([shift_prev[:, None].astype(p.dtype), p[:, :-1]], axis=1)
    m = p + (prev - p) * mu
    r, k, v, pw, pa, pg = jnp.split(m, RWKV_SPLITS, axis=-1)
    w_log = -jax.nn.softplus(-(w0 + jnp.tanh(pw) @ w_decay).astype(jnp.float32)) - 0.5
    decay = jnp.exp(-jnp.exp(w_log))
    a = jax.nn.sigmoid(a0 + pa @ w_iclr)
    g = jax.nn.sigmoid(pg) @ w_gate_out
    if vres is None:
        v_first = v
    else:
        v0, v_res_a, v_res_b = vres
        v = v + (v_first - v) * jax.nn.sigmoid(v0 + (v @ v_res_a) @ v_res_b)
    heads = lambda z: z.reshape(bsz, t, RWKV_H, RWKV_HEAD)
    kk = heads(k * k_k).astype(jnp.float32)
    kk = kk / jnp.maximum(jnp.sqrt(jnp.sum(kk * kk, axis=-1, keepdims=True)), 1e-12)
    k = k * (1 + (a - 1) * k_a)
    y, s_last = wkv7_scan(heads(r), heads(decay), heads(k), heads(v), kk, heads(a), s0)
    mean = jnp.mean(y, axis=-1, keepdims=True)
    var = jnp.mean(jnp.square(y - mean), axis=-1, keepdims=True)
    yn = ((y - mean) * lax.rsqrt(var + GN_EPS)).reshape(bsz, t, RWKV_D) * lnx_w + lnx_b
    bonus = jnp.sum(heads(r) * heads(k) * r_k, axis=-1, keepdims=True) * heads(v)
    out = (yn + bonus.reshape(bsz, t, RWKV_D)) * g
    return out.astype(p.dtype), v_first, s_last.astype(s0.dtype)


def mla_project(p_qa, p_kva, pos, g_qa, w_qb, g_qn, g_qr, g_kva, g_kr):
    bsz, t = p_qa.shape[:2]
    q = (rms_norm(p_qa, g_qa) @ w_qb).reshape(bsz, t, MLA_H, NOPE + ROPE)
    q_nope = rms_norm(q[..., :NOPE], g_qn)
    q_rope = rotary(rms_norm(q[..., NOPE:], g_qr), pos)
    c_kv = rms_norm(p_kva[..., :KV_LORA], g_kva)
    k_rope = rotary(rms_norm(p_kva[..., KV_LORA:], g_kr), pos)
    rows = jnp.concatenate([c_kv, k_rope], axis=-1)
    return q_nope, q_rope, rows


def mla_expand(rows, w_uk, w_uv, g_kn):
    c_kv, k_rope = rows[..., :KV_LORA], rows[..., KV_LORA:]
    k_nope = rms_norm(jnp.einsum('bkc,chd->bkhd', c_kv, w_uk), g_kn)
    v = jnp.einsum('bkc,chd->bkhd', c_kv, w_uv)
    return k_nope, k_rope, v


def mla_attend(q_nope, q_rope, k_nope, k_rope, v, q_pos, k_pos):
    scale = (NOPE + ROPE) ** -0.5
    s = (jnp.einsum('bqhd,bkhd->bhqk', q_nope, k_nope)
         + jnp.einsum('bqhd,bkd->bhqk', q_rope, k_rope)).astype(jnp.float32) * scale
    s = jnp.where(k_pos[None, :] <= q_pos[:, None], s, -jnp.inf)
    pr = jax.nn.softmax(s, axis=-1).astype(v.dtype)
    return jnp.einsum('bhqk,bkhd->bqhd', pr, v)


def mla_prompt(q_nope, q_rope, rows, w_uk, w_uv, g_kn):
    bsz, t = rows.shape[:2]
    k_nope, k_rope, v = mla_expand(rows, w_uk, w_uv, g_kn)
    pos = jnp.arange(t)
    nb = t // Q_BLOCK
    qn_b = jnp.swapaxes(q_nope.reshape(bsz, nb, Q_BLOCK, MLA_H, NOPE), 0, 1)
    qr_b = jnp.swapaxes(q_rope.reshape(bsz, nb, Q_BLOCK, MLA_H, ROPE), 0, 1)
    qp_b = pos.reshape(nb, Q_BLOCK)

    def query_block(args):
        qn, qr, qp = args
        return mla_attend(qn, qr, k_nope, k_rope, v, qp, pos)

    o = lax.map(query_block, (qn_b, qr_b, qp_b))
    return jnp.swapaxes(o, 0, 1).reshape(bsz, t, MLA_H * MLA_V)


def mla_sample(q_nope, q_rope, rows_new, cache_kv, layer, page_table, past_len, w_uk, w_uv, g_kn):
    n_new = rows_new.shape[1]
    q_pos = past_len + jnp.arange(n_new)
    k_pos = jnp.arange(past_len + n_new)

    def one_sequence(args):
        qn, qr, rn, pt = args
        past = cache_kv[pt, layer].reshape(past_len, KV_W)
        rows = jnp.concatenate([past, rn.astype(past.dtype)], axis=0)[None]
        kn, kr, v = mla_expand(rows, w_uk, w_uv, g_kn)
        return mla_attend(qn[None], qr[None], kn, kr, v, q_pos, k_pos)[0]

    o = lax.map(one_sequence, (q_nope, q_rope, rows_new, page_table))
    return o.reshape(o.shape[0], n_new, MLA_H * MLA_V)


def moe_swiglu(h, w_router, w_gate, w_up, w_down):
    n_tok, d = h.shape
    logits = (h @ w_router).astype(jnp.float32)
    top_logit, top_e = lax.top_k(logits, TOP_K)
    gates = jax.nn.softmax(top_logit, axis=-1).astype(h.dtype)
    n_assign = n_tok * TOP_K
    flat_e = top_e.reshape(-1)
    flat_tok = jnp.repeat(jnp.arange(n_tok, dtype=jnp.int32), TOP_K)
    flat_g = gates.reshape(-1)
    order = jnp.argsort(flat_e)
    e_sorted = flat_e[order]
    counts = jnp.bincount(flat_e, length=N_EXPERTS)
    padded = (counts + MOE_BLOCK - 1) // MOE_BLOCK * MOE_BLOCK
    pad_end = jnp.cumsum(padded)
    pad_start = pad_end - padded
    start = jnp.cumsum(counts) - counts
    slot = pad_start[e_sorted] + (jnp.arange(n_assign) - start[e_sorted])
    n_blocks = -(-(n_assign + N_EXPERTS * (MOE_BLOCK - 1)) // MOE_BLOCK)
    n_rows = n_blocks * MOE_BLOCK
    row_tok = jnp.full((n_rows,), n_tok, jnp.int32).at[slot].set(flat_tok[order])
    row_gate = jnp.zeros((n_rows,), h.dtype).at[slot].set(flat_g[order])
    blk_start = jnp.arange(n_blocks) * MOE_BLOCK
    blk_e = jnp.minimum(jnp.sum(pad_end[None, :] <= blk_start[:, None], axis=1), N_EXPERTS - 1)
    h_pad = jnp.concatenate([h, jnp.zeros((1, d), h.dtype)], axis=0)
    xb = h_pad[row_tok].reshape(n_blocks, MOE_BLOCK, d)

    def expert_block(args):
        xblk, e = args
        return swiglu(xblk, w_gate[e], w_up[e], w_down[e])

    yb = lax.map(expert_block, (xb, blk_e)).reshape(n_rows, d)
    out = jnp.zeros((n_tok + 1, d), h.dtype).at[row_tok].add(yb * row_gate[:, None])
    return out[:n_tok]


def setup_inputs(seed: int = 0) -> dict:
    key = jax.random.key(seed)
    ks = iter(jax.random.split(key, 64))
    nrm = lambda shape, scale: jax.random.normal(next(ks), shape, jnp.float32) * scale
    n_pages = PAST_LEN // PAGE_SIZE
    n_used = DEC_BATCH * n_pages
    n_pool = n_used + max(1, n_used // 4)
    page_table = jax.random.permutation(next(ks), n_pool)[:n_used].reshape(DEC_BATCH, n_pages).astype(jnp.int32)
    page_keys = jax.random.split(next(ks), n_pool)
    cache_kv = lax.map(lambda kk: jax.random.normal(kk, (DEPTH, PAGE_SIZE, KV_W), jnp.float32), page_keys)
    nd = max(DEPTH - 1, 0)
    return {
        'x_prompt': nrm((BATCH, SEQ, D_MODEL), 1.0),
        'x_sample': nrm((DEC_BATCH, DEC_SEQ, D_MODEL), 1.0),
        'cache_kv': cache_kv,
        'page_table': page_table,
        'state_wkv': nrm((DEPTH, DEC_BATCH, RWKV_H, RWKV_HEAD, RWKV_HEAD), 0.3),
        'state_shift': nrm((DEPTH, DEC_BATCH, RWKV_W), 1.0),
        'g_mix': 1.0 + nrm((DEPTH, D_MODEL), 0.05),
        'w_in': nrm((DEPTH, D_MODEL, IN_W), D_MODEL ** -0.5),
        'mu_shift': jax.random.uniform(next(ks), (DEPTH, RWKV_W), jnp.float32, 0.1, 0.9),
        'w0': nrm((DEPTH, RWKV_D), 0.5),
        'w_decay': nrm((DEPTH, W_LORA, RWKV_D), 0.5 * W_LORA ** -0.5),
        'a0': nrm((DEPTH, RWKV_D), 0.1),
        'w_iclr': nrm((DEPTH, A_LORA, RWKV_D), A_LORA ** -0.5),
        'w_gate_out': nrm((DEPTH, G_LORA, RWKV_D), G_LORA ** -0.5),
        'k_k': 0.85 + nrm((DEPTH, RWKV_D), 0.05),
        'k_a': 1.0 + nrm((DEPTH, RWKV_D), 0.05),
        'r_k': nrm((DEPTH, RWKV_H, RWKV_HEAD), 0.1),
        'lnx_w': 1.0 + nrm((DEPTH, RWKV_D), 0.05),
        'lnx_b': nrm((DEPTH, RWKV_D), 0.02),
        'v0': nrm((nd, RWKV_D), 0.1),
        'v_res_a': nrm((nd, RWKV_D, V_LORA), RWKV_D ** -0.5),
        'v_res_b': nrm((nd, V_LORA, RWKV_D), V_LORA ** -0.5),
        'g_qa': 1.0 + nrm((DEPTH, Q_LORA), 0.05),
        'w_qb': nrm((DEPTH, Q_LORA, MLA_H * (NOPE + ROPE)), Q_LORA ** -0.5),
        'g_qn': 1.0 + nrm((DEPTH, NOPE), 0.05),
        'g_qr': 1.0 + nrm((DEPTH, ROPE), 0.05),
        'g_kva': 1.0 + nrm((DEPTH, KV_LORA), 0.05),
        'g_kr': 1.0 + nrm((DEPTH, ROPE), 0.05),
        'w_uk': nrm((DEPTH, KV_LORA, MLA_H, NOPE), KV_LORA ** -0.5),
        'w_uv': nrm((DEPTH, KV_LORA, MLA_H, MLA_V), KV_LORA ** -0.5),
        'g_kn': 1.0 + nrm((DEPTH, NOPE), 0.05),
        'w_out': nrm((DEPTH, MIX_W, D_MODEL), MIX_W ** -0.5),
        'g_ffn': 1.0 + nrm((DEPTH, D_MODEL), 0.05),
        'w_ffn_gate': nrm((N_DENSE, D_MODEL, D_FF), D_MODEL ** -0.5),
        'w_ffn_up': nrm((N_DENSE, D_MODEL, D_FF), D_MODEL ** -0.5),
        'w_ffn_down': nrm((N_DENSE, D_FF, D_MODEL), D_FF ** -0.5),
        'w_router': nrm((N_MOE, D_MODEL, N_EXPERTS), D_MODEL ** -0.5),
        'w_exp_gate': nrm((N_MOE, N_EXPERTS, D_MODEL, D_EXP), D_MODEL ** -0.5),
        'w_exp_up': nrm((N_MOE, N_EXPERTS, D_MODEL, D_EXP), D_MODEL ** -0.5),
        'w_exp_down': nrm((N_MOE, N_EXPERTS, D_EXP, D_MODEL), D_EXP ** -0.5),
    }


def reference(x_prompt, x_sample, cache_kv, page_table, state_wkv, state_shift,
              g_mix, w_in, mu_shift, w0, w_decay, a0, w_iclr, w_gate_out, k_k, k_a, r_k,
              lnx_w, lnx_b, v0, v_res_a, v_res_b,
              g_qa, w_qb, g_qn, g_qr, g_kva, g_kr, w_uk, w_uv, g_kn, w_out,
              g_ffn, w_ffn_gate, w_ffn_up, w_ffn_down,
              w_router, w_exp_gate, w_exp_up, w_exp_down):
    past_len = page_table.shape[1] * PAGE_SIZE

    def mixer(x, l, pos, shift_prev, s0, v_first, attend):
        h = rms_norm(x, g_mix[l])
        p = h @ w_in[l]
        p_rwkv = p[..., :RWKV_W]
        p_qa = p[..., RWKV_W:RWKV_W + Q_LORA]
        p_kva = p[..., RWKV_W + Q_LORA:]
        vres = None if l == 0 else (v0[l - 1], v_res_a[l - 1], v_res_b[l - 1])
        o_a, v_first, s_last = rwkv7_time_mix(
            p_rwkv, shift_prev, s0, v_first, mu_shift[l], w0[l], w_decay[l], a0[l], w_iclr[l],
            w_gate_out[l], k_k[l], k_a[l], r_k[l], lnx_w[l], lnx_b[l], vres)
        q_nope, q_rope, rows = mla_project(p_qa, p_kva, pos, g_qa[l], w_qb[l], g_qn[l], g_qr[l],
                                           g_kva[l], g_kr[l])
        o_b = attend(q_nope, q_rope, rows, l)
        y = x + jnp.concatenate([o_a, o_b.astype(o_a.dtype)], axis=-1) @ w_out[l]
        return y, v_first, rows, s_last, p_rwkv[:, -1]

    def channel(x, l):
        h = rms_norm(x, g_ffn[l])
        i = l // 2
        if l % 2 == 0:
            f = swiglu(h, w_ffn_gate[i], w_ffn_up[i], w_ffn_down[i])
        else:
            f = moe_swiglu(h.reshape(-1, h.shape[-1]), w_router[i], w_exp_gate[i], w_exp_up[i],
                           w_exp_down[i]).reshape(h.shape)
        return x + f

    attend_prompt = lambda qn, qr, rows, l: mla_prompt(qn, qr, rows, w_uk[l], w_uv[l], g_kn[l])
    attend_sample = lambda qn, qr, rows, l: mla_sample(qn, qr, rows, cache_kv, l, page_table, past_len,
                                                       w_uk[l], w_uv[l], g_kn[l])

    pos_p = jnp.arange(x_prompt.shape[1], dtype=jnp.int32)
    pos_s = past_len + jnp.arange(x_sample.shape[1], dtype=jnp.int32)
    bp = x_prompt.shape[0]
    shift0 = jnp.zeros((bp, RWKV_W), x_prompt.dtype)
    wkv0 = jnp.zeros((bp, RWKV_H, RWKV_HEAD, RWKV_HEAD), x_prompt.dtype)

    yp, ys, vf_p, vf_s = x_prompt, x_sample, None, None
    rows_p, rows_s, wkv_p, wkv_s, sh_p, sh_s = [], [], [], [], [], []
    for l in range(DEPTH):
        yp, vf_p, rp, wp, shp = mixer(yp, l, pos_p, shift0, wkv0, vf_p, attend_prompt)
        yp = channel(yp, l)
        ys, vf_s, rs, ws, shs = mixer(ys, l, pos_s, state_shift[l], state_wkv[l], vf_s, attend_sample)
        ys = channel(ys, l)
        rows_p.append(rp)
        rows_s.append(rs)
        wkv_p.append(wp)
        wkv_s.append(ws)
        sh_p.append(shp)
        sh_s.append(shs)
    return (yp, ys, jnp.stack(rows_p), jnp.stack(rows_s), jnp.stack(wkv_p), jnp.stack(wkv_s),
            jnp.stack(sh_p), jnp.stack(sh_s))
```

```python
import functools
import math

import jax
import jax.numpy as jnp
from jax import lax
from jax.experimental import pallas as pl
from jax.experimental.pallas import tpu as pltpu

F32 = jnp.float32
BF16 = jnp.bfloat16
RMS_EPS = 1e-6
GN_EPS = 64e-5
ROPE_THETA = 10000.0
TOP_K = 2
PAGE_SIZE = 128
LANES = 128
VMEM_LIMIT = 56 * 1024 * 1024


def _pick(n, prefs):
    for p in prefs:
        if n % p == 0:
            return p
    raise ValueError(f"no tile for {n} in {prefs}")


def _round_up(n, m):
    return (n + m - 1) // m * m


def _params(*sem):
    return pltpu.CompilerParams(dimension_semantics=sem, vmem_limit_bytes=VMEM_LIMIT)


def _dot(a, b, **kw):
    return jnp.dot(a, b, preferred_element_type=F32, **kw)


def _dot_nt(a, b):
    return lax.dot_general(a, b, (((1,), (1,)), ((), ())), preferred_element_type=F32)


def _dot_tn(a, b):
    return lax.dot_general(a, b, (((0,), (0,)), ((), ())), preferred_element_type=F32)


def _rms(x, g, n=None):
    n = x.shape[-1] if n is None else n
    ms = jnp.sum(x * x, axis=-1, keepdims=True) * (1.0 / n)
    return x * lax.rsqrt(ms + RMS_EPS) * g


def _in_proj_kernel(x_ref, g_ref, w_ref, o_ref, hn_ref):
    @pl.when(pl.program_id(1) == 0)
    def _():
        hn_ref[...] = _rms(x_ref[...], g_ref[...]).astype(BF16)

    o_ref[...] = _dot(hn_ref[...], w_ref[...])


def _in_proj(x, g, w):
    m, d = x.shape
    n = w.shape[1]
    tm = _pick(m, (1024, 512, 256, 128, 64, 32, 16, 8))
    tn = _pick(n, (512, 256, 128))
    return pl.pallas_call(
        _in_proj_kernel,
        grid=(m // tm, n // tn),
        in_specs=[pl.BlockSpec((tm, d), lambda i, j: (i, 0)),
                  pl.BlockSpec((1, d), lambda i, j: (0, 0)),
                  pl.BlockSpec((d, tn), lambda i, j: (0, j))],
        out_specs=pl.BlockSpec((tm, tn), lambda i, j: (i, j)),
        out_shape=jax.ShapeDtypeStruct((m, n), F32),
        scratch_shapes=[pltpu.VMEM((tm, d), BF16)],
        compiler_params=_params("parallel", "arbitrary"),
        name="in_proj",
    )(x, g, w)


def _rwkv_kernel(*refs, n_heads, head, chunk, has_vres):
    it = iter(refs)
    p_ref, shift0_ref, s0_ref = next(it), next(it), next(it)
    vfirst_ref = next(it) if has_vres else None
    mu_ref, w0_ref, wdec_ref, a0_ref, wiclr_ref, wgate_ref = (next(it) for _ in range(6))
    kk_ref, ka_ref, rk_ref, lnw_ref, lnb_ref = (next(it) for _ in range(5))
    if has_vres:
        v0_ref, vra_ref, vrb_ref = next(it), next(it), next(it)
    o_ref = next(it)
    vout_ref = None if has_vres else next(it)
    slast_ref = next(it)
    s_scr, carry_scr, y_scr = next(it), next(it), next(it)

    c_idx = pl.program_id(1)
    rd = n_heads * head
    rw = mu_ref.shape[1]
    C = chunk

    @pl.when(c_idx == 0)
    def _():
        s_scr[...] = s0_ref[0]
        carry_scr[...] = shift0_ref[0]

    p = p_ref[:, :rw]
    rid = lax.broadcasted_iota(jnp.int32, p.shape, 0)
    prev = jnp.where(rid == 0, carry_scr[...], pltpu.roll(p, 1, 0))
    carry_scr[...] = p[C - 1:C, :]
    m = p + (prev - p) * mu_ref[...]

    r = m[:, :rd]
    k = m[:, rd:2 * rd]
    v = m[:, 2 * rd:3 * rd]
    pw = m[:, 3 * rd:3 * rd + LANES]
    pa = m[:, 3 * rd + LANES:3 * rd + 2 * LANES]
    pg = m[:, 3 * rd + 2 * LANES:]

    u = w0_ref[...] + _dot(jnp.tanh(pw).astype(BF16), wdec_ref[...])
    lw = (-math.exp(-0.5)) * jax.nn.sigmoid(u)
    a = jax.nn.sigmoid(a0_ref[...] + _dot(pa.astype(BF16), wiclr_ref[...]))
    g = _dot(jax.nn.sigmoid(pg).astype(BF16), wgate_ref[...])
    if has_vres:
        lo = _dot(v.astype(BF16), vra_ref[...])
        v = v + (vfirst_ref[...] - v) * jax.nn.sigmoid(v0_ref[...] + _dot(lo.astype(BF16), vrb_ref[...]))
    else:
        vout_ref[...] = v
    kk_raw = k * kk_ref[...]
    k = k * (1.0 + (a - 1.0) * ka_ref[...])

    row = lax.broadcasted_iota(jnp.int32, (C, C), 0)
    col = lax.broadcasted_iota(jnp.int32, (C, C), 1)
    incl = col <= row
    strict = col < row
    cum = _dot(incl.astype(F32), lw, precision=lax.Precision.HIGHEST)
    dec = jnp.exp(cum)
    dec_prev = jnp.exp(cum - lw)
    dec_inv = jnp.exp(-cum)
    eye = (col == row).astype(F32)

    for h in range(n_heads):
        sl = slice(h * head, (h + 1) * head)
        kk = kk_raw[:, sl]
        kk = kk / jnp.maximum(jnp.sqrt(jnp.sum(kk * kk, axis=-1, keepdims=True)), 1e-12)
        r_h, k_h, v_h = r[:, sl], k[:, sl], v[:, sl]
        a_t = (-kk) * dec_prev[:, sl]
        r_t = r_h * dec[:, sl]
        b_t = (kk * a[:, sl]) * dec_inv[:, sl]
        k_t = k_h * dec_inv[:, sl]
        ar = jnp.concatenate([a_t, r_t], axis=0).astype(BF16)
        b_tb, k_tb = b_t.astype(BF16), k_t.astype(BF16)
        g_b = _dot_nt(ar, b_tb)
        g_k = _dot_nt(ar, k_tb)
        l_ab = jnp.where(strict, g_b[:C], 0.0)
        l_ak = jnp.where(strict, g_k[:C], 0.0)
        m_rb = jnp.where(incl, g_b[C:], 0.0)
        m_rk = jnp.where(incl, g_k[C:], 0.0)
        inv = eye + l_ab
        pw_l = l_ab
        n = 1
        while 2 * n < C:
            pw_b = pw_l.astype(BF16)
            pw_l = _dot(pw_b, pw_b)
            inv = inv + _dot(inv.astype(BF16), pw_l.astype(BF16))
            n *= 2
        s = s_scr[h]
        s_b = s.astype(BF16)
        v_b = v_h.astype(BF16)
        x = _dot_nt(ar[:C], s_b) + _dot(l_ak.astype(BF16), v_b)
        u_h = _dot(inv.astype(BF16), x.astype(BF16))
        u_b = u_h.astype(BF16)
        y = _dot_nt(ar[C:], s_b) + _dot(m_rb.astype(BF16), u_b) + _dot(m_rk.astype(BF16), v_b)
        upd = _dot_tn(u_b, b_tb) + _dot_tn(v_b, k_tb)
        s_scr[h] = (s + upd) * dec[C - 1:C, sl]

        mean = jnp.mean(y, axis=-1, keepdims=True)
        yc = y - mean
        var = jnp.mean(yc * yc, axis=-1, keepdims=True)
        yn = yc * lax.rsqrt(var + GN_EPS) * lnw_ref[:, sl] + lnb_ref[:, sl]
        bonus = jnp.sum(r_h * k_h * rk_ref[:, sl], axis=-1, keepdims=True) * v_h
        y_scr[:, sl] = yn + bonus

    o_ref[...] = (y_scr[...] * g).astype(o_ref.dtype)

    @pl.when(c_idx == pl.num_programs(1) - 1)
    def _():
        slast_ref[0] = s_scr[...]


def _rwkv_mixer(p, row0, bsz, t, shift0, s0, vfirst, prm, chunk):
    n_heads, head = s0.shape[1], s0.shape[2]
    rd = n_heads * head
    rw = prm["mu"].shape[1]
    has_vres = vfirst is not None
    nc = t // chunk
    blk0 = row0 // chunk
    np_ = p.shape[1]

    def full(a):
        return pl.BlockSpec(a.shape, lambda b, c: (0,) * a.ndim)

    ins = [p, shift0, s0]
    specs = [pl.BlockSpec((chunk, np_), lambda b, c: (blk0 + b * nc + c, 0)),
             pl.BlockSpec((1, 1, rw), lambda b, c: (b, 0, 0)),
             pl.BlockSpec((1, n_heads, head, head), lambda b, c: (b, 0, 0, 0))]
    if has_vres:
        ins.append(vfirst)
        specs.append(pl.BlockSpec((chunk, rd), lambda b, c: (b * nc + c, 0)))
    names = ["mu", "w0", "wdec", "a0", "wiclr", "wgate", "k_k", "k_a", "r_k", "lnx_w", "lnx_b"]
    if has_vres:
        names += ["v0", "vra", "vrb"]
    for nm in names:
        ins.append(prm[nm])
        specs.append(full(prm[nm]))

    row_spec = pl.BlockSpec((chunk, rd), lambda b, c: (b * nc + c, 0))
    outs = [jax.ShapeDtypeStruct((bsz * t, rd), BF16)]
    out_specs = [row_spec]
    if not has_vres:
        outs.append(jax.ShapeDtypeStruct((bsz * t, rd), F32))
        out_specs.append(row_spec)
    outs.append(jax.ShapeDtypeStruct(s0.shape, F32))
    out_specs.append(pl.BlockSpec((1, n_heads, head, head), lambda b, c: (b, 0, 0, 0)))

    res = pl.pallas_call(
        functools.partial(_rwkv_kernel, n_heads=n_heads, head=head, chunk=chunk, has_vres=has_vres),
        grid=(bsz, nc),
        in_specs=specs,
        out_specs=out_specs,
        out_shape=outs,
        scratch_shapes=[pltpu.VMEM((n_heads, head, head), F32),
                        pltpu.VMEM((1, rw), F32),
                        pltpu.VMEM((chunk, rd), F32)],
        compiler_params=_params("parallel", "arbitrary"),
        name="rwkv_mixer",
    )(*ins)
    if has_vres:
        o, s_last = res
        return o, vfirst, s_last
    o, v_out, s_last = res
    return o, v_out, s_last


def _rotate(x, cos, sin_a, sin_b):
    half = LANES // 4
    return x * cos + pltpu.roll(x, LANES - half, 1) * sin_a + pltpu.roll(x, half, 1) * sin_b


def _mla_proj_kernel(qa_ref, kv_ref, cos_ref, sina_ref, sinb_ref, gqa_ref, wqb_ref, gqn_ref, gqr_ref,
                     gkva_ref, gkr_ref, q_ref, rows_ref, c_ref, kr_ref, *, n_heads, rope):
    cos, sin_a, sin_b = cos_ref[...], sina_ref[...], sinb_ref[...]
    hq = _rms(qa_ref[...], gqa_ref[...]).astype(BF16)
    q = _dot(hq, wqb_ref[...])
    hw = n_heads * LANES
    for h in range(n_heads):
        qn = _rms(q[:, h * LANES:(h + 1) * LANES], gqn_ref[...])
        qr = _rms(q[:, hw + h * LANES:hw + (h + 1) * LANES], gqr_ref[...], n=rope)
        qr = _rotate(qr, cos, sin_a, sin_b)
        q_ref[:, 2 * h * LANES:(2 * h + 1) * LANES] = qn.astype(q_ref.dtype)
        q_ref[:, (2 * h + 1) * LANES:(2 * h + 2) * LANES] = qr.astype(q_ref.dtype)
    kv = kv_ref[...]
    lat = gkva_ref.shape[1]
    c = _rms(kv[:, :lat], gkva_ref[...])
    kr = _rotate(_rms(kv[:, lat:lat + LANES], gkr_ref[...], n=rope), cos, sin_a, sin_b)
    rows_ref[:, :lat] = c
    rows_ref[:, lat:] = kr[:, :rope]
    c_ref[...] = c.astype(c_ref.dtype)
    kr_ref[...] = kr.astype(kr_ref.dtype)


def _mla_proj(p, col_q, col_kv, tabs, prm, n_heads, rope):
    m = p.shape[0]
    tm = _pick(m, (512, 256, 128, 64, 32, 16, 8))
    qw = prm["g_qa"].shape[1]
    lat = prm["g_kva"].shape[1]
    assert qw == 512 and col_q % qw == 0 and col_kv % qw == 0

    def full(a):
        return pl.BlockSpec(a.shape, lambda i: (0,) * a.ndim)

    tab_spec = pl.BlockSpec((tm, LANES), lambda i: (i, 0))
    ws = [prm["g_qa"], prm["w_qb"], prm["g_qn"], prm["g_qr"], prm["g_kva"], prm["g_kr"]]
    return pl.pallas_call(
        functools.partial(_mla_proj_kernel, n_heads=n_heads, rope=rope),
        grid=(m // tm,),
        in_specs=[pl.BlockSpec((tm, qw), lambda i: (i, col_q // qw)),
                  pl.BlockSpec((tm, qw), lambda i: (i, col_kv // qw)),
                  tab_spec, tab_spec, tab_spec] + [full(a) for a in ws],
        out_specs=[pl.BlockSpec((tm, 2 * n_heads * LANES), lambda i: (i, 0)),
                   pl.BlockSpec((tm, lat + rope), lambda i: (i, 0)),
                   pl.BlockSpec((tm, lat), lambda i: (i, 0)),
                   pl.BlockSpec((tm, LANES), lambda i: (i, 0))],
        out_shape=[jax.ShapeDtypeStruct((m, 2 * n_heads * LANES), BF16),
                   jax.ShapeDtypeStruct((m, lat + rope), F32),
                   jax.ShapeDtypeStruct((m, lat), BF16),
                   jax.ShapeDtypeStruct((m, LANES), BF16)],
        compiler_params=_params("parallel"),
        name="mla_proj",
    )(p, p, *tabs, *ws)


def _kv_expand_kernel(c_ref, kr_ref, wuk_ref, wuv_ref, gkn_ref, k_ref, v_ref, *, n_heads):
    c = c_ref[...]
    ke = _dot(c, wuk_ref[...])
    kr = kr_ref[...]
    for h in range(n_heads):
        kh = _rms(ke[:, h * LANES:(h + 1) * LANES], gkn_ref[...])
        k_ref[:, 2 * h * LANES:(2 * h + 1) * LANES] = kh.astype(k_ref.dtype)
        k_ref[:, (2 * h + 1) * LANES:(2 * h + 2) * LANES] = kr
    v_ref[...] = _dot(c, wuv_ref[...]).astype(v_ref.dtype)


def _kv_expand(c, kr, mp, wuk, wuv, gkn, n_heads):
    lat = c.shape[1]
    tm = _pick(mp, (512, 256, 128, 64, 32, 16, 8))
    assert wuk.shape[1] == n_heads * LANES and wuv.shape[1] == n_heads * LANES

    def full(a):
        return pl.BlockSpec(a.shape, lambda i: (0,) * a.ndim)

    return pl.pallas_call(
        functools.partial(_kv_expand_kernel, n_heads=n_heads),
        grid=(mp // tm,),
        in_specs=[pl.BlockSpec((tm, lat), lambda i: (i, 0)),
                  pl.BlockSpec((tm, LANES), lambda i: (i, 0)),
                  full(wuk), full(wuv), full(gkn)],
        out_specs=[pl.BlockSpec((tm, 2 * n_heads * LANES), lambda i: (i, 0)),
                   pl.BlockSpec((tm, n_heads * LANES), lambda i: (i, 0))],
        out_shape=[jax.ShapeDtypeStruct((mp, 2 * n_heads * LANES), BF16),
                   jax.ShapeDtypeStruct((mp, n_heads * LANES), BF16)],
        compiler_params=_params("parallel"),
        name="kv_expand",
    )(c, kr, wuk, wuv, gkn)


def _flash_kernel(q_ref, k_ref, v_ref, o_ref, m_scr, l_scr, acc_scr, *, scale, tile):
    qi, ki = pl.program_id(2), pl.program_id(3)

    @pl.when(ki == 0)
    def _():
        m_scr[...] = jnp.full(m_scr.shape, -jnp.inf, F32)
        l_scr[...] = jnp.zeros(l_scr.shape, F32)
        acc_scr[...] = jnp.zeros(acc_scr.shape, F32)

    @pl.when(ki <= qi)
    def _():
        s = _dot_nt(q_ref[...], k_ref[...]) * scale
        row = qi * tile + lax.broadcasted_iota(jnp.int32, s.shape, 0)
        col = ki * tile + lax.broadcasted_iota(jnp.int32, s.shape, 1)
        s = jnp.where(col <= row, s, -jnp.inf)
        m_old = m_scr[...]
        m_new = jnp.maximum(m_old, jnp.max(s, axis=-1, keepdims=True))
        alpha = jnp.exp(m_old - m_new)
        p = jnp.exp(s - m_new)
        l_scr[...] = alpha * l_scr[...] + jnp.sum(p, axis=-1, keepdims=True)
        acc_scr[...] = alpha * acc_scr[...] + _dot(p.astype(BF16), v_ref[...])
        m_scr[...] = m_new

    @pl.when(ki == pl.num_programs(3) - 1)
    def _():
        o_ref[...] = (acc_scr[...] / l_scr[...]).astype(o_ref.dtype)


def _flash_prompt(q, k, v, bsz, t, n_heads, scale):
    tile = _pick(t, (512, 256, 128))
    nt = t // tile
    return pl.pallas_call(
        functools.partial(_flash_kernel, scale=scale, tile=tile),
        grid=(bsz, n_heads, nt, nt),
        in_specs=[pl.BlockSpec((tile, 2 * LANES), lambda b, h, qi, ki: (b * nt + qi, h)),
                  pl.BlockSpec((tile, 2 * LANES), lambda b, h, qi, ki: (b * nt + jnp.minimum(ki, qi), h)),
                  pl.BlockSpec((tile, LANES), lambda b, h, qi, ki: (b * nt + jnp.minimum(ki, qi), h))],
        out_specs=pl.BlockSpec((tile, LANES), lambda b, h, qi, ki: (b * nt + qi, h)),
        out_shape=jax.ShapeDtypeStruct((bsz * t, n_heads * LANES), BF16),
        scratch_shapes=[pltpu.VMEM((tile, 1), F32), pltpu.VMEM((tile, 1), F32), pltpu.VMEM((tile, LANES), F32)],
        compiler_params=_params("parallel", "parallel", "parallel", "arbitrary"),
        name="flash_prompt",
    )(q, k, v)


def _sample_attn_kernel(pt_ref, q_ref, cn_ref, krn_ref, wuk3_ref, wukt_ref, wuv3_ref, gkn_ref, *rest,
                        n_heads, n_tok, n_page_in, sub_pages, lat, rope, scale):
    page_refs = rest[:n_page_in]
    o_ref = rest[n_page_in]
    qlat_scr, qr_scr, m_scr, l_scr, acc_scr = rest[n_page_in + 1:]
    j = pl.program_id(1)
    nq = n_heads * n_tok

    @pl.when(j == 0)
    def _():
        for h in range(n_heads):
            qn = (q_ref[:, 2 * h * LANES:(2 * h + 1) * LANES].astype(F32) * gkn_ref[...]).astype(BF16)
            qlat_scr[h * n_tok:(h + 1) * n_tok, :] = _dot_nt(qn, wuk3_ref[h]).astype(BF16)
            qr_scr[h * n_tok:(h + 1) * n_tok, :] = q_ref[:, (2 * h + 1) * LANES:(2 * h + 2) * LANES]
        m_scr[...] = jnp.full(m_scr.shape, -jnp.inf, F32)
        l_scr[...] = jnp.zeros(l_scr.shape, F32)
        acc_scr[...] = jnp.zeros(acc_scr.shape, F32)

    def attend(c, kr, mask):
        npos = c.shape[0]
        ket = _dot_nt(wukt_ref[...], c)
        invs = []
        for h in range(n_heads):
            kh = ket[h * LANES:(h + 1) * LANES]
            ms = jnp.sum(kh * kh, axis=0, keepdims=True) * (1.0 / LANES)
            invs.append(jnp.broadcast_to(lax.rsqrt(ms + RMS_EPS), (n_tok, npos)))
        inv = jnp.concatenate(invs, axis=0)
        s = (_dot_nt(qlat_scr[...], c) * inv + _dot_nt(qr_scr[:, :rope], kr)) * scale
        if mask is not None:
            s = jnp.where(mask, s, -jnp.inf)
        m_old = m_scr[...]
        m_new = jnp.maximum(m_old, jnp.max(s, axis=-1, keepdims=True))
        alpha = jnp.exp(m_old - m_new)
        p = jnp.exp(s - m_new)
        l_scr[...] = alpha * l_scr[...] + jnp.sum(p, axis=-1, keepdims=True)
        acc_scr[...] = alpha * acc_scr[...] + _dot(p.astype(BF16), c)
        m_scr[...] = m_new

    for sub in range(n_page_in // sub_pages):
        rows = jnp.concatenate([page_refs[sub * sub_pages + i][0, 0] for i in range(sub_pages)], axis=0)
        attend(rows[:, :lat].astype(BF16), rows[:, lat:lat + rope].astype(BF16), None)

    @pl.when(j == pl.num_programs(1) - 1)
    def _():
        row = lax.broadcasted_iota(jnp.int32, (nq, n_tok), 0)
        col = lax.broadcasted_iota(jnp.int32, (nq, n_tok), 1)
        attend(cn_ref[...], krn_ref[:, :rope], col <= (row % n_tok))
        o_lat = (acc_scr[...] / l_scr[...]).astype(BF16)
        for h in range(n_heads):
            o_h = _dot(o_lat[h * n_tok:(h + 1) * n_tok], wuv3_ref[h])
            o_ref[:, h * LANES:(h + 1) * LANES] = o_h.astype(o_ref.dtype)


def _sample_attn(q, c_new, kr_new, row0, cache_kv, layer, page_table, wuk3, wukt, wuv3, gkn,
                 n_tok, n_heads, rope, scale):
    bs, n_pages = page_table.shape
    lat = c_new.shape[1]
    kvw = cache_kv.shape[-1]
    page = cache_kv.shape[2]
    n_page_in = _pick(n_pages, (16, 8, 4, 2, 1))
    sub_pages = _pick(n_page_in, (4, 2, 1))
    steps = n_pages // n_page_in
    blk0 = row0 // n_tok
    nq = n_heads * n_tok

    def full(a):
        return pl.BlockSpec(a.shape, lambda b, j, pt: (0,) * a.ndim)

    def page_spec(i):
        return pl.BlockSpec((1, 1, page, kvw),
                            lambda b, j, pt: (pt[b * n_pages + j * n_page_in + i], layer, 0, 0))

    grid_spec = pltpu.PrefetchScalarGridSpec(
        num_scalar_prefetch=1,
        grid=(bs, steps),
        in_specs=[pl.BlockSpec((n_tok, 2 * n_heads * LANES), lambda b, j, pt: (blk0 + b, 0)),
                  pl.BlockSpec((n_tok, lat), lambda b, j, pt: (blk0 + b, 0)),
                  pl.BlockSpec((n_tok, LANES), lambda b, j, pt: (blk0 + b, 0)),
                  full(wuk3), full(wukt), full(wuv3), full(gkn)] + [page_spec(i) for i in range(n_page_in)],
        out_specs=pl.BlockSpec((n_tok, n_heads * LANES), lambda b, j, pt: (b, 0)),
        scratch_shapes=[pltpu.VMEM((nq, lat), BF16), pltpu.VMEM((nq, LANES), BF16),
                        pltpu.VMEM((nq, 1), F32), pltpu.VMEM((nq, 1), F32), pltpu.VMEM((nq, lat), F32)],
    )
    return pl.pallas_call(
        functools.partial(_sample_attn_kernel, n_heads=n_heads, n_tok=n_tok, n_page_in=n_page_in,
                          sub_pages=sub_pages, lat=lat, rope=rope, scale=scale),
        grid_spec=grid_spec,
        out_shape=jax.ShapeDtypeStruct((bs * n_tok, n_heads * LANES), BF16),
        compiler_params=_params("parallel", "arbitrary"),
        name="sample_attn",
    )(page_table.reshape(-1), q, c_new, kr_new, wuk3, wukt, wuv3, gkn, *([cache_kv] * n_page_in))


def _out_proj_kernel(x_ref, a_ref, b_ref, w_ref, o_ref):
    ka = a_ref.shape[1]
    o_ref[...] = x_ref[...] + _dot(a_ref[...], w_ref[:ka, :]) + _dot(b_ref[...], w_ref[ka:, :])


def _out_proj(x, a, b, w):
    m, d = x.shape
    tm = _pick(m, (1024, 512, 256, 128, 64, 32, 16, 8))
    tn = _pick(d, (512, 256, 128))
    return pl.pallas_call(
        _out_proj_kernel,
        grid=(m // tm, d // tn),
        in_specs=[pl.BlockSpec((tm, tn), lambda i, j: (i, j)),
                  pl.BlockSpec((tm, a.shape[1]), lambda i, j: (i, 0)),
                  pl.BlockSpec((tm, b.shape[1]), lambda i, j: (i, 0)),
                  pl.BlockSpec((w.shape[0], tn), lambda i, j: (0, j))],
        out_specs=pl.BlockSpec((tm, tn), lambda i, j: (i, j)),
        out_shape=jax.ShapeDtypeStruct((m, d), F32),
        compiler_params=_params("parallel", "parallel"),
        name="out_proj",
    )(x, a, b, w)


def _ffn_kernel(x_ref, g_ref, wg_ref, wu_ref, wd_ref, o_ref, hn_ref):
    @pl.when(pl.program_id(1) == 0)
    def _():
        x = x_ref[...]
        hn_ref[...] = _rms(x, g_ref[...]).astype(BF16)
        o_ref[...] = x

    h = hn_ref[...]
    gt = _dot(h, wg_ref[...])
    act = (gt * jax.nn.sigmoid(gt) * _dot(h, wu_ref[...])).astype(BF16)
    o_ref[...] += _dot(act, wd_ref[...])


def _ffn_dense(x, g, wg, wu, wd):
    m, d = x.shape
    f = wg.shape[1]
    tm = _pick(m, (512, 256, 128, 64, 32, 16, 8))
    tf = _pick(f, (512, 256, 128))
    return pl.pallas_call(
        _ffn_kernel,
        grid=(m // tm, f // tf),
        in_specs=[pl.BlockSpec((tm, d), lambda i, j: (i, 0)),
                  pl.BlockSpec((1, d), lambda i, j: (0, 0)),
                  pl.BlockSpec((d, tf), lambda i, j: (0, j)),
                  pl.BlockSpec((d, tf), lambda i, j: (0, j)),
                  pl.BlockSpec((tf, d), lambda i, j: (j, 0))],
        out_specs=pl.BlockSpec((tm, d), lambda i, j: (i, 0)),
        out_shape=jax.ShapeDtypeStruct((m, d), F32),
        scratch_shapes=[pltpu.VMEM((tm, d), BF16)],
        compiler_params=_params("parallel", "arbitrary"),
        name="ffn_dense",
    )(x, g, wg, wu, wd)


def _router_kernel(x_ref, g_ref, wr_ref, hn_ref, lg_ref):
    hn = _rms(x_ref[...], g_ref[...])
    hn_ref[...] = hn.astype(hn_ref.dtype)
    lg_ref[...] = _dot(hn, wr_ref[...], precision=lax.Precision.HIGHEST)


def _router(x, g, wr):
    m, d = x.shape
    tm = _pick(m, (512, 256, 128, 64, 32, 16, 8))
    return pl.pallas_call(
        _router_kernel,
        grid=(m // tm,),
        in_specs=[pl.BlockSpec((tm, d), lambda i: (i, 0)),
                  pl.BlockSpec((1, d), lambda i: (0, 0)),
                  pl.BlockSpec(wr.shape, lambda i: (0, 0))],
        out_specs=[pl.BlockSpec((tm, d), lambda i: (i, 0)),
                   pl.BlockSpec((tm, wr.shape[1]), lambda i: (i, 0))],
        out_shape=[jax.ShapeDtypeStruct((m, d), BF16),
                   jax.ShapeDtypeStruct((m, wr.shape[1]), F32)],
        compiler_params=_params("parallel"),
        name="router",
    )(x, g, wr)


def _experts_kernel(blk_e_ref, n_used_ref, xs_ref, gate_ref, wg_ref, wu_ref, wd_ref, o_ref):
    i, j = pl.program_id(0), pl.program_id(1)

    @pl.when(j == 0)
    def _():
        o_ref[...] = jnp.zeros(o_ref.shape, o_ref.dtype)

    @pl.when(i < n_used_ref[0])
    def _():
        x = xs_ref[...]
        gt = _dot(x, wg_ref[0])
        act = (gt * jax.nn.sigmoid(gt) * _dot(x, wu_ref[0])).astype(BF16)
        o_ref[...] += _dot(act, wd_ref[0])

    @pl.when(j == pl.num_programs(1) - 1)
    def _():
        o_ref[...] = o_ref[...] * gate_ref[...]


def _experts(xs, row_gate, blk_e, n_used, wg, wu, wd, bm):
    n_rows, d = xs.shape
    f = wg.shape[2]
    tf = _pick(f, (256, 128))
    grid_spec = pltpu.PrefetchScalarGridSpec(
        num_scalar_prefetch=2,
        grid=(n_rows // bm, f // tf),
        in_specs=[pl.BlockSpec((bm, d), lambda i, j, be, nu: (i, 0)),
                  pl.BlockSpec((bm, 1), lambda i, j, be, nu: (i, 0)),
                  pl.BlockSpec((1, d, tf), lambda i, j, be, nu: (be[i], 0, j)),
                  pl.BlockSpec((1, d, tf), lambda i, j, be, nu: (be[i], 0, j)),
                  pl.BlockSpec((1, tf, d), lambda i, j, be, nu: (be[i], j, 0))],
        out_specs=pl.BlockSpec((bm, d), lambda i, j, be, nu: (i, 0)),
    )
    return pl.pallas_call(
        _experts_kernel,
        grid_spec=grid_spec,
        out_shape=jax.ShapeDtypeStruct((n_rows, d), F32),
        compiler_params=_params("parallel", "arbitrary"),
        name="experts",
    )(blk_e, n_used, xs, row_gate, wg, wu, wd)


def _moe(x, g, wr, wg, wu, wd):
    m, d = x.shape
    n_exp = wg.shape[0]
    wr_pad = jnp.pad(wr, ((0, 0), (0, LANES - n_exp)))
    hn, logits = _router(x, g, wr_pad)
    top_logit, top_e = lax.top_k(logits[:, :n_exp], TOP_K)
    gates = jax.nn.softmax(top_logit, axis=-1)
    n_assign = m * TOP_K
    bm = _pick(m, (512, 256, 128, 64, 32, 16, 8))
    flat_e = top_e.reshape(-1)
    onehot = (flat_e[:, None] == jnp.arange(n_exp, dtype=flat_e.dtype)[None, :]).astype(jnp.int32)
    csum = jnp.cumsum(onehot, axis=0)
    counts = csum[-1]
    rank = jnp.sum((csum - onehot) * onehot, axis=1)
    padded = (counts + bm - 1) // bm * bm
    pad_end = jnp.cumsum(padded)
    pad_start = pad_end - padded
    slot = pad_start[flat_e] + rank
    n_rows = _round_up(n_assign + n_exp * (bm - 1), bm)
    n_blocks = n_rows // bm
    flat_tok = jnp.repeat(jnp.arange(m, dtype=jnp.int32), TOP_K)
    row_tok = jnp.full((n_rows,), m, jnp.int32).at[slot].set(flat_tok)
    row_gate = jnp.zeros((n_rows,), F32).at[slot].set(gates.reshape(-1))
    blk_start = jnp.arange(n_blocks, dtype=jnp.int32) * bm
    blk_e = jnp.minimum(jnp.sum(pad_end[None, :] <= blk_start[:, None], axis=1), n_exp - 1).astype(jnp.int32)
    n_used = (pad_end[-1] // bm).astype(jnp.int32).reshape(1)
    hn_pad = jnp.concatenate([hn, jnp.zeros((1, d), hn.dtype)], axis=0)
    xs = hn_pad[row_tok]
    yb = _experts(xs, row_gate[:, None], blk_e, n_used, wg, wu, wd, bm)
    slot2 = slot.reshape(m, TOP_K)
    return x + yb[slot2[:, 0]] + yb[slot2[:, 1]]


def _pad_cols(a, width):
    return jnp.pad(a, [(0, 0)] * (a.ndim - 1) + [(0, width - a.shape[-1])])


def _rwkv_cols(a, rd, wl, al):
    o = 3 * rd
    return jnp.concatenate([a[..., :o], _pad_cols(a[..., o:o + wl], LANES),
                            _pad_cols(a[..., o + wl:o + wl + al], LANES),
                            _pad_cols(a[..., o + wl + al:], 2 * LANES)], axis=-1)


def _rwkv_cols_inv(a, rd, wl, al, gl):
    o = 3 * rd
    return jnp.concatenate([a[..., :o], a[..., o:o + wl], a[..., o + LANES:o + LANES + al],
                            a[..., o + 2 * LANES:o + 2 * LANES + gl]], axis=-1)


def _pad_rows(a, rows):
    return jnp.pad(a, ((0, rows - a.shape[0]), (0, 0)))


def _rope_tables(pos, rope):
    half = rope // 2
    assert half == LANES // 4
    inv_freq = ROPE_THETA ** (-jnp.arange(half, dtype=F32) * (2.0 / rope))
    ang = pos.astype(F32)[:, None] * inv_freq[None, :]
    cos, sin = jnp.cos(ang), jnp.sin(ang)
    z = jnp.zeros_like(cos)
    return (jnp.concatenate([cos, cos, z, z], axis=1),
            jnp.concatenate([-sin, z, z, z], axis=1),
            jnp.concatenate([z, sin, z, z], axis=1))


def kernel(x_prompt, x_sample, cache_kv, page_table, state_wkv, state_shift, g_mix, w_in, mu_shift, w0, w_decay, a0, w_iclr, w_gate_out, k_k, k_a, r_k, lnx_w, lnx_b, v0, v_res_a, v_res_b, g_qa, w_qb, g_qn, g_qr, g_kva, g_kr, w_uk, w_uv, g_kn, w_out, g_ffn, w_ffn_gate, w_ffn_up, w_ffn_down, w_router, w_exp_gate, w_exp_up, w_exp_down):
    bp, tp, d = x_prompt.shape
    bs, ts, _ = x_sample.shape
    depth = g_mix.shape[0]
    n_heads_r, head_r = r_k.shape[1], r_k.shape[2]
    rd = n_heads_r * head_r
    wl, al, gl = w_decay.shape[1], w_iclr.shape[1], w_gate_out.shape[1]
    rw_orig = 3 * rd + wl + al + gl
    ql, lat, rope, nope = g_qa.shape[1], g_kva.shape[1], g_kr.shape[1], g_qn.shape[1]
    n_heads_a, mla_v = w_uk.shape[2], w_uv.shape[3]
    assert nope == LANES and mla_v == LANES and rope == LANES // 2
    assert wl <= LANES and al <= LANES and gl <= 2 * LANES and v_res_a.shape[2] <= LANES
    past_len = page_table.shape[1] * PAGE_SIZE
    mp, ms = bp * tp, bs * ts
    scale = (nope + rope) ** -0.5
    rw = 3 * rd + 4 * LANES
    col_q, col_kv = rw, rw + ql
    chunk_p = _pick(tp, (64, 32, 16, 8))
    chunk_s = _pick(ts, (64, 32, 16, 8))

    x = jnp.concatenate([x_prompt.reshape(mp, d), x_sample.reshape(ms, d)], axis=0)
    pos = jnp.concatenate([jnp.tile(jnp.arange(tp, dtype=jnp.int32), bp),
                           jnp.tile(past_len + jnp.arange(ts, dtype=jnp.int32), bs)])
    tabs = _rope_tables(pos, rope)
    row2 = lambda a: a.reshape(1, -1)

    shift0_p = jnp.zeros((bp, 1, rw), F32)
    wkv0_p = jnp.zeros((bp, n_heads_r, head_r, head_r), F32)
    vf_p = vf_s = None
    rows_p, rows_s, wkv_p, wkv_s, sh_p, sh_s = [], [], [], [], [], []
    for l in range(depth):
        w_in_l = w_in[l]
        w_pad = jnp.concatenate([
            _rwkv_cols(w_in_l[:, :rw_orig], rd, wl, al),
            w_in_l[:, rw_orig:rw_orig + ql + lat],
            _pad_cols(w_in_l[:, rw_orig + ql + lat:], 2 * LANES)], axis=1).astype(BF16)
        p = _in_proj(x, row2(g_mix[l]), w_pad)

        prm = {
            "mu": _rwkv_cols(row2(mu_shift[l]), rd, wl, al),
            "w0": row2(w0[l]), "a0": row2(a0[l]),
            "wdec": _pad_rows(w_decay[l], LANES).astype(BF16),
            "wiclr": _pad_rows(w_iclr[l], LANES).astype(BF16),
            "wgate": _pad_rows(w_gate_out[l], 2 * LANES).astype(BF16),
            "k_k": row2(k_k[l]), "k_a": row2(k_a[l]), "r_k": row2(r_k[l]),
            "lnx_w": row2(lnx_w[l]), "lnx_b": row2(lnx_b[l]),
        }
        if l > 0:
            prm["v0"] = row2(v0[l - 1])
            prm["vra"] = _pad_cols(v_res_a[l - 1], LANES).astype(BF16)
            prm["vrb"] = _pad_rows(v_res_b[l - 1], LANES).astype(BF16)
        oa_p, vf_p, wp = _rwkv_mixer(p, 0, bp, tp, shift0_p, wkv0_p, vf_p, prm, chunk_p)
        shift0_s = _rwkv_cols(state_shift[l], rd, wl, al)[:, None, :]
        oa_s, vf_s, ws = _rwkv_mixer(p, mp, bs, ts, shift0_s, state_wkv[l], vf_s, prm, chunk_s)

        wqb3 = w_qb[l].reshape(ql, n_heads_a, nope + rope)
        wqb_pad = jnp.concatenate([wqb3[:, :, :nope].reshape(ql, n_heads_a * nope),
                                   _pad_cols(wqb3[:, :, nope:], LANES).reshape(ql, n_heads_a * LANES)],
                                  axis=1).astype(BF16)
        mprm = {"g_qa": row2(g_qa[l]), "w_qb": wqb_pad, "g_qn": row2(g_qn[l]),
                "g_qr": _pad_cols(row2(g_qr[l]), LANES), "g_kva": row2(g_kva[l]),
                "g_kr": _pad_cols(row2(g_kr[l]), LANES)}
        q_cat, rows, c_bf, kr_bf = _mla_proj(p, col_q, col_kv, tabs, mprm, n_heads_a, rope)

        wuk2 = w_uk[l].reshape(lat, n_heads_a * nope).astype(BF16)
        wuv2 = w_uv[l].reshape(lat, n_heads_a * mla_v).astype(BF16)
        gkn = row2(g_kn[l])
        k_cat, v_exp = _kv_expand(c_bf, kr_bf, mp, wuk2, wuv2, gkn, n_heads_a)
        ob_p = _flash_prompt(q_cat, k_cat, v_exp, bp, tp, n_heads_a, scale)

        wuk3 = jnp.transpose(w_uk[l], (1, 0, 2)).astype(BF16)
        wukt = jnp.transpose(w_uk[l], (1, 2, 0)).reshape(n_heads_a * nope, lat).astype(BF16)
        wuv3 = jnp.transpose(w_uv[l], (1, 0, 2)).astype(BF16)
        ob_s = _sample_attn(q_cat, c_bf, kr_bf, mp, cache_kv, l, page_table, wuk3, wukt, wuv3, gkn,
                            ts, n_heads_a, rope, scale)

        o_a = jnp.concatenate([oa_p, oa_s], axis=0)
        o_b = jnp.concatenate([ob_p, ob_s], axis=0)
        x = _out_proj(x, o_a, o_b, w_out[l].astype(BF16))

        i = l // 2
        if l % 2 == 0:
            x = _ffn_dense(x, row2(g_ffn[l]), w_ffn_gate[i].astype(BF16), w_ffn_up[i].astype(BF16),
                           w_ffn_down[i].astype(BF16))
        else:
            x = _moe(x, row2(g_ffn[l]), w_router[i], w_exp_gate[i].astype(BF16), w_exp_up[i].astype(BF16),
                     w_exp_down[i].astype(BF16))

        rows_p.append(rows[:mp].reshape(bp, tp, lat + rope))
        rows_s.append(rows[mp:].reshape(bs, ts, lat + rope))
        wkv_p.append(wp)
        wkv_s.append(ws)
        last_p = p[:mp].reshape(bp, tp, -1)[:, -1, :rw]
        last_s = p[mp:].reshape(bs, ts, -1)[:, -1, :rw]
        sh_p.append(_rwkv_cols_inv(last_p, rd, wl, al, gl))
        sh_s.append(_rwkv_cols_inv(last_s, rd, wl, al, gl))

    return (x[:mp].reshape(bp, tp, d), x[mp:].reshape(bs, ts, d),
            jnp.stack(rows_p), jnp.stack(rows_s), jnp.stack(wkv_p), jnp.stack(wkv_s),
            jnp.stack(sh_p), jnp.stack(sh_s))
```

```python
import functools
import math

import jax
import jax.numpy as jnp
from jax import lax
from jax.experimental import pallas as pl
from jax.experimental.pallas import tpu as pltpu

F32 = jnp.float32
BF16 = jnp.bfloat16
RMS_EPS = 1e-6
GN_EPS = 64e-5
ROPE_THETA = 10000.0
TOP_K = 2
PAGE_SIZE = 128
LANES = 128
VMEM_LIMIT = 56 * 1024 * 1024


def _pick(n, prefs):
    for p in prefs:
        if n % p == 0:
            return p
    raise ValueError(f"no tile for {n} in {prefs}")


def _round_up(n, m):
    return (n + m - 1) // m * m


def _params(*sem):
    return pltpu.CompilerParams(dimension_semantics=sem, vmem_limit_bytes=VMEM_LIMIT)


def _dot(a, b, **kw):
    return jnp.dot(a, b, preferred_element_type=F32, **kw)


def _dot_nt(a, b):
    return lax.dot_general(a, b, (((1,), (1,)), ((), ())), preferred_element_type=F32)


def _dot_tn(a, b):
    return lax.dot_general(a, b, (((0,), (0,)), ((), ())), preferred_element_type=F32)


def _rms(x, g, n=None):
    n = x.shape[-1] if n is None else n
    ms = jnp.sum(x * x, axis=-1, keepdims=True) * (1.0 / n)
    return x * lax.rsqrt(ms + RMS_EPS) * g


def _in_proj_kernel(x_ref, g_ref, w_ref, o_ref, hn_ref):
    @pl.when(pl.program_id(1) == 0)
    def _():
        hn_ref[...] = _rms(x_ref[...], g_ref[...]).astype(BF16)

    o_ref[...] = _dot(hn_ref[...], w_ref[...])


def _in_proj(x, g, w):
    m, d = x.shape
    n = w.shape[1]
    tm = _pick(m, (1024, 512, 256, 128, 64, 32, 16, 8))
    tn = _pick(n, (512, 256, 128))
    return pl.pallas_call(
        _in_proj_kernel,
        grid=(m // tm, n // tn),
        in_specs=[pl.BlockSpec((tm, d), lambda i, j: (i, 0)),
                  pl.BlockSpec((1, d), lambda i, j: (0, 0)),
                  pl.BlockSpec((d, tn), lambda i, j: (0, j))],
        out_specs=pl.BlockSpec((tm, tn), lambda i, j: (i, j)),
        out_shape=jax.ShapeDtypeStruct((m, n), F32),
        scratch_shapes=[pltpu.VMEM((tm, d), BF16)],
        compiler_params=_params("parallel", "arbitrary"),
        name="in_proj",
    )(x, g, w)


def _rwkv_kernel(*refs, n_heads, head, chunk, has_vres, head_group):
    it = iter(refs)
    p_ref, shift0_ref, s0_ref = next(it), next(it), next(it)
    vfirst_ref = next(it) if has_vres else None
    mu_ref, w0_ref, wdec_ref, a0_ref, wiclr_ref, wgate_ref = (next(it) for _ in range(6))
    kk_ref, ka_ref, rk_ref, lnw_ref, lnb_ref = (next(it) for _ in range(5))
    if has_vres:
        v0_ref, vra_ref, vrb_ref = next(it), next(it), next(it)
    o_ref = next(it)
    vout_ref = None if has_vres else next(it)
    slast_ref = next(it)
    s_scr, carry_scr, y_scr = next(it), next(it), next(it)

    c_idx = pl.program_id(1)
    rd = n_heads * head
    rw = mu_ref.shape[1]
    C = chunk

    @pl.when(c_idx == 0)
    def _():
        s_scr[...] = s0_ref[0]
        carry_scr[...] = shift0_ref[0]

    p = p_ref[:, :rw]
    rid = lax.broadcasted_iota(jnp.int32, p.shape, 0)
    prev = jnp.where(rid == 0, carry_scr[...], pltpu.roll(p, 1, 0))
    carry_scr[...] = p[C - 1:C, :]
    m = p + (prev - p) * mu_ref[...]

    r = m[:, :rd]
    k = m[:, rd:2 * rd]
    v = m[:, 2 * rd:3 * rd]
    pw = m[:, 3 * rd:3 * rd + LANES]
    pa = m[:, 3 * rd + LANES:3 * rd + 2 * LANES]
    pg = m[:, 3 * rd + 2 * LANES:]

    u = w0_ref[...] + _dot(jnp.tanh(pw).astype(BF16), wdec_ref[...])
    lw = (-math.exp(-0.5)) * jax.nn.sigmoid(u)
    a = jax.nn.sigmoid(a0_ref[...] + _dot(pa.astype(BF16), wiclr_ref[...]))
    g = _dot(jax.nn.sigmoid(pg).astype(BF16), wgate_ref[...])
    if has_vres:
        lo = _dot(v.astype(BF16), vra_ref[...])
        v = v + (vfirst_ref[...] - v) * jax.nn.sigmoid(v0_ref[...] + _dot(lo.astype(BF16), vrb_ref[...]))
    else:
        vout_ref[...] = v
    kk_raw = k * kk_ref[...]
    k = k * (1.0 + (a - 1.0) * ka_ref[...])

    row = lax.broadcasted_iota(jnp.int32, (C, C), 0)
    col = lax.broadcasted_iota(jnp.int32, (C, C), 1)
    incl = col <= row
    strict = col < row
    cum = _dot(incl.astype(F32), lw, precision=lax.Precision.HIGHEST)
    dec = jnp.exp(cum)
    dec_prev = jnp.exp(cum - lw)
    dec_inv = jnp.exp(-cum)
    eye = (col == row).astype(F32)

    mask2 = jnp.concatenate([strict, incl], axis=0)

    for h0 in range(0, n_heads, head_group):
        hs = range(h0, min(h0 + head_group, n_heads))
        sls = {h: slice(h * head, (h + 1) * head) for h in hs}
        ar, bk, vb, s_old = {}, {}, {}, {}
        for h in hs:
            sl = sls[h]
            kk = kk_raw[:, sl]
            kk = kk / jnp.maximum(jnp.sqrt(jnp.sum(kk * kk, axis=-1, keepdims=True)), 1e-12)
            a_t = (-kk) * dec_prev[:, sl]
            r_t = r[:, sl] * dec[:, sl]
            b_t = (kk * a[:, sl]) * dec_inv[:, sl]
            k_t = k[:, sl] * dec_inv[:, sl]
            ar[h] = jnp.concatenate([a_t, r_t], axis=0).astype(BF16)
            bk[h] = (b_t.astype(BF16), k_t.astype(BF16))
            vb[h] = v[:, sl].astype(BF16)
            s_old[h] = s_scr[h]
        g_b = {h: jnp.where(mask2, _dot_nt(ar[h], bk[h][0]), 0.0) for h in hs}
        g_k = {h: jnp.where(mask2, _dot_nt(ar[h], bk[h][1]), 0.0).astype(BF16) for h in hs}
        pw_l = {h: g_b[h][:C] for h in hs}
        inv = {h: eye + pw_l[h] for h in hs}
        n = 1
        while 2 * n < C:
            pw_b = {h: pw_l[h].astype(BF16) for h in hs}
            pw_l = {h: _dot(pw_b[h], pw_b[h]) for h in hs}
            inv = {h: inv[h] + _dot(inv[h].astype(BF16), pw_l[h].astype(BF16)) for h in hs}
            n *= 2
        xy = {h: _dot_nt(ar[h], s_old[h].astype(BF16)) + _dot(g_k[h], vb[h]) for h in hs}
        u_b = {h: _dot(inv[h].astype(BF16), xy[h][:C].astype(BF16)).astype(BF16) for h in hs}
        y = {h: xy[h][C:] + _dot(g_b[h][C:].astype(BF16), u_b[h]) for h in hs}
        for h in hs:
            sl = sls[h]
            upd = _dot_tn(jnp.concatenate([u_b[h], vb[h]], axis=0),
                          jnp.concatenate(bk[h], axis=0))
            s_scr[h] = (s_old[h] + upd) * dec[C - 1:C, sl]
            mean = jnp.mean(y[h], axis=-1, keepdims=True)
            yc = y[h] - mean
            var = jnp.mean(yc * yc, axis=-1, keepdims=True)
            yn = yc * lax.rsqrt(var + GN_EPS) * lnw_ref[:, sl] + lnb_ref[:, sl]
            bonus = jnp.sum(r[:, sl] * k[:, sl] * rk_ref[:, sl], axis=-1, keepdims=True) * v[:, sl]
            y_scr[:, sl] = yn + bonus

    o_ref[...] = (y_scr[...] * g).astype(o_ref.dtype)

    @pl.when(c_idx == pl.num_programs(1) - 1)
    def _():
        slast_ref[0] = s_scr[...]


def _rwkv_mixer(p, row0, bsz, t, shift0, s0, vfirst, prm, chunk):
    n_heads, head = s0.shape[1], s0.shape[2]
    rd = n_heads * head
    rw = prm["mu"].shape[1]
    has_vres = vfirst is not None
    nc = t // chunk
    blk0 = row0 // chunk
    np_ = p.shape[1]

    def full(a):
        return pl.BlockSpec(a.shape, lambda b, c: (0,) * a.ndim)

    ins = [p, shift0, s0]
    specs = [pl.BlockSpec((chunk, np_), lambda b, c: (blk0 + b * nc + c, 0)),
             pl.BlockSpec((1, 1, rw), lambda b, c: (b, 0, 0)),
             pl.BlockSpec((1, n_heads, head, head), lambda b, c: (b, 0, 0, 0))]
    if has_vres:
        ins.append(vfirst)
        specs.append(pl.BlockSpec((chunk, rd), lambda b, c: (b * nc + c, 0)))
    names = ["mu", "w0", "wdec", "a0", "wiclr", "wgate", "k_k", "k_a", "r_k", "lnx_w", "lnx_b"]
    if has_vres:
        names += ["v0", "vra", "vrb"]
    for nm in names:
        ins.append(prm[nm])
        specs.append(full(prm[nm]))

    row_spec = pl.BlockSpec((chunk, rd), lambda b, c: (b * nc + c, 0))
    outs = [jax.ShapeDtypeStruct((bsz * t, rd), BF16)]
    out_specs = [row_spec]
    if not has_vres:
        outs.append(jax.ShapeDtypeStruct((bsz * t, rd), F32))
        out_specs.append(row_spec)
    outs.append(jax.ShapeDtypeStruct(s0.shape, F32))
    out_specs.append(pl.BlockSpec((1, n_heads, head, head), lambda b, c: (b, 0, 0, 0)))

    res = pl.pallas_call(
        functools.partial(_rwkv_kernel, n_heads=n_heads, head=head, chunk=chunk, has_vres=has_vres,
                          head_group=min(n_heads, 8)),
        grid=(bsz, nc),
        in_specs=specs,
        out_specs=out_specs,
        out_shape=outs,
        scratch_shapes=[pltpu.VMEM((n_heads, head, head), F32),
                        pltpu.VMEM((1, rw), F32),
                        pltpu.VMEM((chunk, rd), F32)],
        compiler_params=_params("parallel", "arbitrary"),
        name="rwkv_mixer",
    )(*ins)
    if has_vres:
        o, s_last = res
        return o, vfirst, s_last
    o, v_out, s_last = res
    return o, v_out, s_last


def _rotate(x, cos, sin_a, sin_b):
    half = LANES // 4
    return x * cos + pltpu.roll(x, LANES - half, 1) * sin_a + pltpu.roll(x, half, 1) * sin_b


def _mla_proj_kernel(qa_ref, kv_ref, cos_ref, sina_ref, sinb_ref, gqa_ref, wqb_ref, gqn_ref, gqr_ref,
                     gkva_ref, gkr_ref, q_ref, rows_ref, c_ref, kr_ref, *, n_heads, rope, scale):
    cos, sin_a, sin_b = cos_ref[...], sina_ref[...], sinb_ref[...]
    hq = _rms(qa_ref[...], gqa_ref[...]).astype(BF16)
    q = _dot(hq, wqb_ref[...])
    hw = n_heads * LANES
    for h in range(n_heads):
        qn = _rms(q[:, h * LANES:(h + 1) * LANES], gqn_ref[...]) * scale
        qr = _rms(q[:, hw + h * LANES:hw + (h + 1) * LANES], gqr_ref[...], n=rope)
        qr = _rotate(qr, cos, sin_a, sin_b) * scale
        q_ref[:, 2 * h * LANES:(2 * h + 1) * LANES] = qn.astype(q_ref.dtype)
        q_ref[:, (2 * h + 1) * LANES:(2 * h + 2) * LANES] = qr.astype(q_ref.dtype)
    kv = kv_ref[...]
    lat = gkva_ref.shape[1]
    c = _rms(kv[:, :lat], gkva_ref[...])
    kr = _rotate(_rms(kv[:, lat:lat + LANES], gkr_ref[...], n=rope), cos, sin_a, sin_b)
    rows_ref[:, :lat] = c
    rows_ref[:, lat:] = kr[:, :rope]
    c_ref[...] = c.astype(c_ref.dtype)
    kr_ref[...] = kr.astype(kr_ref.dtype)


def _mla_proj(p, col_q, col_kv, tabs, prm, n_heads, rope, scale):
    m = p.shape[0]
    tm = _pick(m, (512, 256, 128, 64, 32, 16, 8))
    qw = prm["g_qa"].shape[1]
    lat = prm["g_kva"].shape[1]
    assert qw == 512 and col_q % qw == 0 and col_kv % qw == 0

    def full(a):
        return pl.BlockSpec(a.shape, lambda i: (0,) * a.ndim)

    tab_spec = pl.BlockSpec((tm, LANES), lambda i: (i, 0))
    ws = [prm["g_qa"], prm["w_qb"], prm["g_qn"], prm["g_qr"], prm["g_kva"], prm["g_kr"]]
    return pl.pallas_call(
        functools.partial(_mla_proj_kernel, n_heads=n_heads, rope=rope, scale=scale),
        grid=(m // tm,),
        in_specs=[pl.BlockSpec((tm, qw), lambda i: (i, col_q // qw)),
                  pl.BlockSpec((tm, qw), lambda i: (i, col_kv // qw)),
                  tab_spec, tab_spec, tab_spec] + [full(a) for a in ws],
        out_specs=[pl.BlockSpec((tm, 2 * n_heads * LANES), lambda i: (i, 0)),
                   pl.BlockSpec((tm, lat + rope), lambda i: (i, 0)),
                   pl.BlockSpec((tm, lat), lambda i: (i, 0)),
                   pl.BlockSpec((tm, LANES), lambda i: (i, 0))],
        out_shape=[jax.ShapeDtypeStruct((m, 2 * n_heads * LANES), BF16),
                   jax.ShapeDtypeStruct((m, lat + rope), F32),
                   jax.ShapeDtypeStruct((m, lat), BF16),
                   jax.ShapeDtypeStruct((m, LANES), BF16)],
        compiler_params=_params("parallel"),
        name="mla_proj",
    )(p, p, *tabs, *ws)


def _kv_expand_kernel(c_ref, kr_ref, wuk_ref, wuv_ref, gkn_ref, k_ref, v_ref, *, n_heads):
    c = c_ref[...]
    ke = _dot(c, wuk_ref[...])
    kr = kr_ref[...]
    for h in range(n_heads):
        kh = _rms(ke[:, h * LANES:(h + 1) * LANES], gkn_ref[...])
        k_ref[:, 2 * h * LANES:(2 * h + 1) * LANES] = kh.astype(k_ref.dtype)
        k_ref[:, (2 * h + 1) * LANES:(2 * h + 2) * LANES] = kr
    v_ref[...] = _dot_nt(wuv_ref[...], c).astype(v_ref.dtype)


def _kv_expand(c, kr, mp, wuk, wuv_t, gkn, n_heads):
    lat = c.shape[1]
    tm = _pick(mp, (512, 256, 128))
    assert wuk.shape[1] == n_heads * LANES and wuv_t.shape[0] == n_heads * LANES

    def full(a):
        return pl.BlockSpec(a.shape, lambda i: (0,) * a.ndim)

    return pl.pallas_call(
        functools.partial(_kv_expand_kernel, n_heads=n_heads),
        grid=(mp // tm,),
        in_specs=[pl.BlockSpec((tm, lat), lambda i: (i, 0)),
                  pl.BlockSpec((tm, LANES), lambda i: (i, 0)),
                  full(wuk), full(wuv_t), full(gkn)],
        out_specs=[pl.BlockSpec((tm, 2 * n_heads * LANES), lambda i: (i, 0)),
                   pl.BlockSpec((n_heads * LANES, tm), lambda i: (0, i))],
        out_shape=[jax.ShapeDtypeStruct((mp, 2 * n_heads * LANES), BF16),
                   jax.ShapeDtypeStruct((n_heads * LANES, mp), BF16)],
        compiler_params=_params("parallel"),
        name="kv_expand",
    )(c, kr, wuk, wuv_t, gkn)


def _flash_kernel(qi_ref, ki_ref, q_ref, k_ref, v_ref, o_ref, m_scr, l_scr, acc_scr, *, tq, tk):
    pair = pl.program_id(2)
    qi, ki = qi_ref[pair], ki_ref[pair]

    @pl.when(ki == 0)
    def _():
        m_scr[...] = jnp.full(m_scr.shape, -jnp.inf, F32)
        l_scr[...] = jnp.zeros(l_scr.shape, F32)
        acc_scr[...] = jnp.zeros(acc_scr.shape, F32)

    s = _dot_nt(k_ref[...], q_ref[...])
    key = ki * tk + lax.broadcasted_iota(jnp.int32, s.shape, 0)
    qry = qi * tq + lax.broadcasted_iota(jnp.int32, s.shape, 1)
    s = jnp.where(key <= qry, s, -jnp.inf)
    m_old = m_scr[...]
    m_new = jnp.maximum(m_old, jnp.max(s, axis=0, keepdims=True))
    alpha = jnp.exp(m_old - m_new)
    p = jnp.exp(s - m_new)
    l_scr[...] = alpha * l_scr[...] + jnp.sum(p, axis=0, keepdims=True)
    acc_scr[...] = alpha * acc_scr[...] + _dot(v_ref[...], p.astype(BF16))
    m_scr[...] = m_new

    @pl.when((ki + 1) * tk >= (qi + 1) * tq)
    def _():
        o_ref[...] = (acc_scr[...] / l_scr[...]).T.astype(o_ref.dtype)


def _flash_prompt(q, k, v, bsz, t, n_heads):
    tk = _pick(t, (512, 256, 128))
    tq = 2 * tk if t % (2 * tk) == 0 else tk
    nq, nk = t // tq, t // tk
    pairs = [(qi, ki) for qi in range(nq) for ki in range((qi + 1) * tq // tk)]
    qi_arr = jnp.asarray([pr[0] for pr in pairs], jnp.int32)
    ki_arr = jnp.asarray([pr[1] for pr in pairs], jnp.int32)
    grid_spec = pltpu.PrefetchScalarGridSpec(
        num_scalar_prefetch=2,
        grid=(bsz, n_heads, len(pairs)),
        in_specs=[pl.BlockSpec((tq, 2 * LANES), lambda b, h, pr, qa, ka: (b * nq + qa[pr], h)),
                  pl.BlockSpec((tk, 2 * LANES), lambda b, h, pr, qa, ka: (b * nk + ka[pr], h)),
                  pl.BlockSpec((LANES, tk), lambda b, h, pr, qa, ka: (h, b * nk + ka[pr]))],
        out_specs=pl.BlockSpec((tq, LANES), lambda b, h, pr, qa, ka: (b * nq + qa[pr], h)),
        scratch_shapes=[pltpu.VMEM((1, tq), F32), pltpu.VMEM((1, tq), F32), pltpu.VMEM((LANES, tq), F32)],
    )
    return pl.pallas_call(
        functools.partial(_flash_kernel, tq=tq, tk=tk),
        grid_spec=grid_spec,
        out_shape=jax.ShapeDtypeStruct((bsz * t, n_heads * LANES), BF16),
        compiler_params=_params("parallel", "parallel", "arbitrary"),
        name="flash_prompt",
    )(qi_arr, ki_arr, q, k, v)


def _sample_attn_kernel(pt_ref, q_ref, cn_ref, krn_ref, wuk3_ref, wukt_ref, wuv3_ref, gkn_ref, *rest,
                        n_heads, n_tok, n_page_in, sub_pages, lat, rope):
    page_refs = rest[:n_page_in]
    o_ref = rest[n_page_in]
    lhs_scr, qr_scr, m_scr, l_scr, acc_scr = rest[n_page_in + 1:]
    j = pl.program_id(1)
    nq = n_heads * n_tok
    hk = n_heads * LANES

    @pl.when(j == 0)
    def _():
        lhs_scr[:hk, :] = wukt_ref[...]
        for h in range(n_heads):
            qn = (q_ref[:, 2 * h * LANES:(2 * h + 1) * LANES].astype(F32) * gkn_ref[...]).astype(BF16)
            lhs_scr[hk + h * n_tok:hk + (h + 1) * n_tok, :] = _dot_nt(qn, wuk3_ref[h]).astype(BF16)
            qr_scr[h * n_tok:(h + 1) * n_tok, :] = q_ref[:, (2 * h + 1) * LANES:(2 * h + 2) * LANES]
        m_scr[...] = jnp.full(m_scr.shape, -jnp.inf, F32)
        l_scr[...] = jnp.zeros(l_scr.shape, F32)
        acc_scr[...] = jnp.zeros(acc_scr.shape, F32)

    def scores(big, s_rope):
        npos = big.shape[1]
        invs = []
        for h in range(n_heads):
            kh = big[h * LANES:(h + 1) * LANES]
            ms = jnp.sum(kh * kh, axis=0, keepdims=True) * (1.0 / LANES)
            invs.append(jnp.broadcast_to(lax.rsqrt(ms + RMS_EPS), (n_tok, npos)))
        return big[hk:] * jnp.concatenate(invs, axis=0) + s_rope

    def update(s, pv):
        m_old = m_scr[...]
        m_new = jnp.maximum(m_old, jnp.max(s, axis=-1, keepdims=True))
        alpha = jnp.exp(m_old - m_new)
        p = jnp.exp(s - m_new)
        l_scr[...] = alpha * l_scr[...] + jnp.sum(p, axis=-1, keepdims=True)
        acc_scr[...] = alpha * acc_scr[...] + pv(p.astype(BF16))
        m_scr[...] = m_new

    c_ts, s_parts = [], []
    for sub in range(n_page_in // sub_pages):
        pages = [page_refs[sub * sub_pages + i][0, 0] for i in range(sub_pages)]
        c_t = jnp.concatenate([pg[:lat].astype(BF16) for pg in pages], axis=1)
        kr_t = jnp.concatenate([pg[lat:lat + rope].astype(BF16) for pg in pages], axis=1)
        c_ts.append(c_t)
        s_parts.append(scores(_dot(lhs_scr[...], c_t), _dot(qr_scr[:, :rope], kr_t)))
    npos = c_ts[0].shape[1]

    def pv_pages(p):
        out = _dot_nt(p[:, :npos], c_ts[0])
        for i in range(1, len(c_ts)):
            out = out + _dot_nt(p[:, i * npos:(i + 1) * npos], c_ts[i])
        return out

    update(jnp.concatenate(s_parts, axis=1), pv_pages)

    @pl.when(j == pl.num_programs(1) - 1)
    def _():
        c_n = cn_ref[...]
        row = lax.broadcasted_iota(jnp.int32, (nq, n_tok), 0)
        col = lax.broadcasted_iota(jnp.int32, (nq, n_tok), 1)
        s = scores(_dot_nt(lhs_scr[...], c_n), _dot_nt(qr_scr[:, :rope], krn_ref[:, :rope]))
        update(jnp.where(col <= (row % n_tok), s, -jnp.inf), lambda p: _dot(p, c_n))
        o_lat = (acc_scr[...] / l_scr[...]).astype(BF16)
        for h in range(n_heads):
            o_h = _dot(o_lat[h * n_tok:(h + 1) * n_tok], wuv3_ref[h])
            o_ref[:, h * LANES:(h + 1) * LANES] = o_h.astype(o_ref.dtype)


def _sample_attn(q, c_new, kr_new, row0, cache_t, layer, page_table, wuk3, wukt, wuv3, gkn,
                 n_tok, n_heads, rope):
    bs, n_pages = page_table.shape
    lat = c_new.shape[1]
    kvw, page = cache_t.shape[2], cache_t.shape[3]
    n_page_in = _pick(n_pages, (16, 8, 4, 2, 1))
    sub_pages = _pick(n_page_in, (4, 2, 1))
    steps = n_pages // n_page_in
    blk0 = row0 // n_tok
    nq = n_heads * n_tok

    def full(a):
        return pl.BlockSpec(a.shape, lambda b, j, pt: (0,) * a.ndim)

    def page_spec(i):
        return pl.BlockSpec((1, 1, kvw, page),
                            lambda b, j, pt: (pt[b * n_pages + j * n_page_in + i], layer, 0, 0))

    grid_spec = pltpu.PrefetchScalarGridSpec(
        num_scalar_prefetch=1,
        grid=(bs, steps),
        in_specs=[pl.BlockSpec((n_tok, 2 * n_heads * LANES), lambda b, j, pt: (blk0 + b, 0)),
                  pl.BlockSpec((n_tok, lat), lambda b, j, pt: (blk0 + b, 0)),
                  pl.BlockSpec((n_tok, LANES), lambda b, j, pt: (blk0 + b, 0)),
                  full(wuk3), full(wukt), full(wuv3), full(gkn)] + [page_spec(i) for i in range(n_page_in)],
        out_specs=pl.BlockSpec((n_tok, n_heads * LANES), lambda b, j, pt: (b, 0)),
        scratch_shapes=[pltpu.VMEM((n_heads * LANES + nq, lat), BF16), pltpu.VMEM((nq, LANES), BF16),
                        pltpu.VMEM((nq, 1), F32), pltpu.VMEM((nq, 1), F32), pltpu.VMEM((nq, lat), F32)],
    )
    return pl.pallas_call(
        functools.partial(_sample_attn_kernel, n_heads=n_heads, n_tok=n_tok, n_page_in=n_page_in,
                          sub_pages=sub_pages, lat=lat, rope=rope),
        grid_spec=grid_spec,
        out_shape=jax.ShapeDtypeStruct((bs * n_tok, n_heads * LANES), BF16),
        compiler_params=_params("parallel", "arbitrary"),
        name="sample_attn",
    )(page_table.reshape(-1), q, c_new, kr_new, wuk3, wukt, wuv3, gkn, *([cache_t] * n_page_in))


def _out_proj_kernel(x_ref, a_ref, b_ref, w_ref, o_ref):
    ka = a_ref.shape[1]
    o_ref[...] = x_ref[...] + _dot(a_ref[...], w_ref[:ka, :]) + _dot(b_ref[...], w_ref[ka:, :])


def _out_proj(x, a, b, w):
    m, d = x.shape
    tm = _pick(m, (1024, 512, 256, 128, 64, 32, 16, 8))
    tn = _pick(d, (512, 256, 128))
    return pl.pallas_call(
        _out_proj_kernel,
        grid=(m // tm, d // tn),
        in_specs=[pl.BlockSpec((tm, tn), lambda i, j: (i, j)),
                  pl.BlockSpec((tm, a.shape[1]), lambda i, j: (i, 0)),
                  pl.BlockSpec((tm, b.shape[1]), lambda i, j: (i, 0)),
                  pl.BlockSpec((w.shape[0], tn), lambda i, j: (0, j))],
        out_specs=pl.BlockSpec((tm, tn), lambda i, j: (i, j)),
        out_shape=jax.ShapeDtypeStruct((m, d), F32),
        compiler_params=_params("parallel", "parallel"),
        name="out_proj",
    )(x, a, b, w)


def _ffn_kernel(x_ref, g_ref, wg_ref, wu_ref, wd_ref, o_ref, hn_ref):
    @pl.when(pl.program_id(1) == 0)
    def _():
        x = x_ref[...]
        hn_ref[...] = _rms(x, g_ref[...]).astype(BF16)
        o_ref[...] = x

    h = hn_ref[...]
    gt = _dot(h, wg_ref[...])
    act = (gt * jax.nn.sigmoid(gt) * _dot(h, wu_ref[...])).astype(BF16)
    o_ref[...] += _dot(act, wd_ref[...])


def _ffn_dense(x, g, wg, wu, wd):
    m, d = x.shape
    f = wg.shape[1]
    tm = _pick(m, (512, 256, 128, 64, 32, 16, 8))
    tf = _pick(f, (512, 256, 128))
    return pl.pallas_call(
        _ffn_kernel,
        grid=(m // tm, f // tf),
        in_specs=[pl.BlockSpec((tm, d), lambda i, j: (i, 0)),
                  pl.BlockSpec((1, d), lambda i, j: (0, 0)),
                  pl.BlockSpec((d, tf), lambda i, j: (0, j)),
                  pl.BlockSpec((d, tf), lambda i, j: (0, j)),
                  pl.BlockSpec((tf, d), lambda i, j: (j, 0))],
        out_specs=pl.BlockSpec((tm, d), lambda i, j: (i, 0)),
        out_shape=jax.ShapeDtypeStruct((m, d), F32),
        scratch_shapes=[pltpu.VMEM((tm, d), BF16)],
        compiler_params=_params("parallel", "arbitrary"),
        name="ffn_dense",
    )(x, g, wg, wu, wd)


def _router_kernel(x_ref, g_ref, wr_ref, hn_ref, lg_ref):
    hn = _rms(x_ref[...], g_ref[...])
    hn_ref[...] = hn.astype(hn_ref.dtype)
    lg_ref[...] = _dot(hn, wr_ref[...], precision=lax.Precision.HIGHEST)


def _router(x, g, wr):
    m, d = x.shape
    tm = _pick(m, (512, 256, 128, 64, 32, 16, 8))
    return pl.pallas_call(
        _router_kernel,
        grid=(m // tm,),
        in_specs=[pl.BlockSpec((tm, d), lambda i: (i, 0)),
                  pl.BlockSpec((1, d), lambda i: (0, 0)),
                  pl.BlockSpec(wr.shape, lambda i: (0, 0))],
        out_specs=[pl.BlockSpec((tm, d), lambda i: (i, 0)),
                   pl.BlockSpec((tm, wr.shape[1]), lambda i: (i, 0))],
        out_shape=[jax.ShapeDtypeStruct((m, d), BF16),
                   jax.ShapeDtypeStruct((m, wr.shape[1]), F32)],
        compiler_params=_params("parallel"),
        name="router",
    )(x, g, wr)


def _experts_kernel(blk_e_ref, n_used_ref, xs_ref, gate_ref, wg_ref, wu_ref, wd_ref, o_ref):
    i, j = pl.program_id(0), pl.program_id(1)

    @pl.when(j == 0)
    def _():
        o_ref[...] = jnp.zeros(o_ref.shape, o_ref.dtype)

    @pl.when(i < n_used_ref[0])
    def _():
        x = xs_ref[...]
        gt = _dot(x, wg_ref[0])
        act = (gt * jax.nn.sigmoid(gt) * _dot(x, wu_ref[0])).astype(BF16)
        o_ref[...] += _dot(act, wd_ref[0])

    @pl.when(j == pl.num_programs(1) - 1)
    def _():
        o_ref[...] = o_ref[...] * gate_ref[...]


def _experts(xs, row_gate, blk_e, n_used, wg, wu, wd, bm):
    n_rows, d = xs.shape
    f = wg.shape[2]
    tf = _pick(f, (256, 128))
    grid_spec = pltpu.PrefetchScalarGridSpec(
        num_scalar_prefetch=2,
        grid=(n_rows // bm, f // tf),
        in_specs=[pl.BlockSpec((bm, d), lambda i, j, be, nu: (i, 0)),
                  pl.BlockSpec((bm, 1), lambda i, j, be, nu: (i, 0)),
                  pl.BlockSpec((1, d, tf), lambda i, j, be, nu: (be[i], 0, j)),
                  pl.BlockSpec((1, d, tf), lambda i, j, be, nu: (be[i], 0, j)),
                  pl.BlockSpec((1, tf, d), lambda i, j, be, nu: (be[i], j, 0))],
        out_specs=pl.BlockSpec((bm, d), lambda i, j, be, nu: (i, 0)),
    )
    return pl.pallas_call(
        _experts_kernel,
        grid_spec=grid_spec,
        out_shape=jax.ShapeDtypeStruct((n_rows, d), F32),
        compiler_params=_params("parallel", "arbitrary"),
        name="experts",
    )(blk_e, n_used, xs, row_gate, wg, wu, wd)


def _moe(x, g, wr, wg, wu, wd, splits):
    m, d = x.shape
    n_exp = wg.shape[0]
    wr_pad = jnp.pad(wr, ((0, 0), (0, LANES - n_exp)))
    hn, logits = _router(x, g, wr_pad)
    top_logit, top_e = lax.top_k(logits[:, :n_exp], TOP_K)
    gates = jax.nn.softmax(top_logit, axis=-1)
    n_assign = m * TOP_K
    bm = _pick(m, (512, 256, 128, 64, 32, 16, 8))
    flat_e = top_e.reshape(-1)
    onehot = (flat_e[:, None] == jnp.arange(n_exp, dtype=flat_e.dtype)[None, :]).astype(jnp.int32)
    csum = jnp.cumsum(onehot, axis=0)
    counts = csum[-1]
    rank = jnp.sum((csum - onehot) * onehot, axis=1)
    padded = (counts + bm - 1) // bm * bm
    pad_end = jnp.cumsum(padded)
    pad_start = pad_end - padded
    slot = pad_start[flat_e] + rank
    n_rows = _round_up(n_assign + n_exp * (bm - 1), bm)
    n_blocks = n_rows // bm
    flat_tok = jnp.repeat(jnp.arange(m, dtype=jnp.int32), TOP_K)
    row_tok = jnp.full((n_rows,), m, jnp.int32).at[slot].set(flat_tok)
    row_gate = jnp.zeros((n_rows,), F32).at[slot].set(gates.reshape(-1))
    blk_start = jnp.arange(n_blocks, dtype=jnp.int32) * bm
    blk_e = jnp.minimum(jnp.sum(pad_end[None, :] <= blk_start[:, None], axis=1), n_exp - 1).astype(jnp.int32)
    n_used = (pad_end[-1] // bm).astype(jnp.int32).reshape(1)
    hn_pad = jnp.concatenate([hn, jnp.zeros((1, d), hn.dtype)], axis=0)
    xs = hn_pad[row_tok]
    yb = _experts(xs, row_gate[:, None], blk_e, n_used, wg, wu, wd, bm)
    slot2 = slot.reshape(m, TOP_K)
    return [x[a:b] + yb[slot2[a:b, 0]] + yb[slot2[a:b, 1]] for a, b in splits]


def _pad_cols(a, width):
    return jnp.pad(a, [(0, 0)] * (a.ndim - 1) + [(0, width - a.shape[-1])])


def _rwkv_cols(a, rd, wl, al):
    o = 3 * rd
    return jnp.concatenate([a[..., :o], _pad_cols(a[..., o:o + wl], LANES),
                            _pad_cols(a[..., o + wl:o + wl + al], LANES),
                            _pad_cols(a[..., o + wl + al:], 2 * LANES)], axis=-1)


def _rwkv_cols_inv(a, rd, wl, al, gl):
    o = 3 * rd
    return jnp.concatenate([a[..., :o], a[..., o:o + wl], a[..., o + LANES:o + LANES + al],
                            a[..., o + 2 * LANES:o + 2 * LANES + gl]], axis=-1)


def _pad_rows(a, rows):
    return jnp.pad(a, ((0, rows - a.shape[0]), (0, 0)))


def _rope_tables(pos, rope):
    half = rope // 2
    assert half == LANES // 4
    inv_freq = ROPE_THETA ** (-jnp.arange(half, dtype=F32) * (2.0 / rope))
    ang = pos.astype(F32)[:, None] * inv_freq[None, :]
    cos, sin = jnp.cos(ang), jnp.sin(ang)
    z = jnp.zeros_like(cos)
    return (jnp.concatenate([cos, cos, z, z], axis=1),
            jnp.concatenate([-sin, z, z, z], axis=1),
            jnp.concatenate([z, sin, z, z], axis=1))


def kernel(x_prompt, x_sample, cache_kv, page_table, state_wkv, state_shift, g_mix, w_in, mu_shift, w0, w_decay, a0, w_iclr, w_gate_out, k_k, k_a, r_k, lnx_w, lnx_b, v0, v_res_a, v_res_b, g_qa, w_qb, g_qn, g_qr, g_kva, g_kr, w_uk, w_uv, g_kn, w_out, g_ffn, w_ffn_gate, w_ffn_up, w_ffn_down, w_router, w_exp_gate, w_exp_up, w_exp_down):
    bp, tp, d = x_prompt.shape
    bs, ts, _ = x_sample.shape
    depth = g_mix.shape[0]
    n_heads_r, head_r = r_k.shape[1], r_k.shape[2]
    rd = n_heads_r * head_r
    wl, al, gl = w_decay.shape[1], w_iclr.shape[1], w_gate_out.shape[1]
    rw_orig = 3 * rd + wl + al + gl
    ql, lat, rope, nope = g_qa.shape[1], g_kva.shape[1], g_kr.shape[1], g_qn.shape[1]
    n_heads_a, mla_v = w_uk.shape[2], w_uv.shape[3]
    assert nope == LANES and mla_v == LANES and rope == LANES // 2
    assert wl <= LANES and al <= LANES and gl <= 2 * LANES and v_res_a.shape[2] <= LANES
    past_len = page_table.shape[1] * PAGE_SIZE
    mp, ms = bp * tp, bs * ts
    scale = (nope + rope) ** -0.5
    rw = 3 * rd + 4 * LANES
    col_q, col_kv = rw, rw + ql
    chunk_p = _pick(tp, (64, 32, 16, 8))
    chunk_s = _pick(ts, (64, 32, 16, 8))

    x = jnp.concatenate([x_prompt.reshape(mp, d), x_sample.reshape(ms, d)], axis=0)
    pos = jnp.concatenate([jnp.tile(jnp.arange(tp, dtype=jnp.int32), bp),
                           jnp.tile(past_len + jnp.arange(ts, dtype=jnp.int32), bs)])
    tabs = _rope_tables(pos, rope)
    row2 = lambda a: a.reshape(1, -1)
    cache_t = jnp.transpose(cache_kv, (0, 1, 3, 2))

    shift0_p = jnp.zeros((bp, 1, rw), F32)
    wkv0_p = jnp.zeros((bp, n_heads_r, head_r, head_r), F32)
    vf_p = vf_s = None
    rows_p, rows_s, wkv_p, wkv_s, sh_p, sh_s = [], [], [], [], [], []
    for l in range(depth):
        w_in_l = w_in[l]
        w_pad = jnp.concatenate([
            _rwkv_cols(w_in_l[:, :rw_orig], rd, wl, al),
            w_in_l[:, rw_orig:rw_orig + ql + lat],
            _pad_cols(w_in_l[:, rw_orig + ql + lat:], 2 * LANES)], axis=1).astype(BF16)
        p = _in_proj(x, row2(g_mix[l]), w_pad)

        prm = {
            "mu": _rwkv_cols(row2(mu_shift[l]), rd, wl, al),
            "w0": row2(w0[l]), "a0": row2(a0[l]),
            "wdec": _pad_rows(w_decay[l], LANES).astype(BF16),
            "wiclr": _pad_rows(w_iclr[l], LANES).astype(BF16),
            "wgate": _pad_rows(w_gate_out[l], 2 * LANES).astype(BF16),
            "k_k": row2(k_k[l]), "k_a": row2(k_a[l]), "r_k": row2(r_k[l]),
            "lnx_w": row2(lnx_w[l]), "lnx_b": row2(lnx_b[l]),
        }
        if l > 0:
            prm["v0"] = row2(v0[l - 1])
            prm["vra"] = _pad_cols(v_res_a[l - 1], LANES).astype(BF16)
            prm["vrb"] = _pad_rows(v_res_b[l - 1], LANES).astype(BF16)
        oa_p, vf_p, wp = _rwkv_mixer(p, 0, bp, tp, shift0_p, wkv0_p, vf_p, prm, chunk_p)
        shift0_s = _rwkv_cols(state_shift[l], rd, wl, al)[:, None, :]
        oa_s, vf_s, ws = _rwkv_mixer(p, mp, bs, ts, shift0_s, state_wkv[l], vf_s, prm, chunk_s)

        wqb3 = w_qb[l].reshape(ql, n_heads_a, nope + rope)
        wqb_pad = jnp.concatenate([wqb3[:, :, :nope].reshape(ql, n_heads_a * nope),
                                   _pad_cols(wqb3[:, :, nope:], LANES).reshape(ql, n_heads_a * LANES)],
                                  axis=1).astype(BF16)
        mprm = {"g_qa": row2(g_qa[l]), "w_qb": wqb_pad, "g_qn": row2(g_qn[l]),
                "g_qr": _pad_cols(row2(g_qr[l]), LANES), "g_kva": row2(g_kva[l]),
                "g_kr": _pad_cols(row2(g_kr[l]), LANES)}
        q_cat, rows, c_bf, kr_bf = _mla_proj(p, col_q, col_kv, tabs, mprm, n_heads_a, rope, scale)

        wuk2 = w_uk[l].reshape(lat, n_heads_a * nope).astype(BF16)
        wuv_t = jnp.transpose(w_uv[l], (1, 2, 0)).reshape(n_heads_a * mla_v, lat).astype(BF16)
        gkn = row2(g_kn[l])
        k_cat, v_exp = _kv_expand(c_bf, kr_bf, mp, wuk2, wuv_t, gkn, n_heads_a)
        ob_p = _flash_prompt(q_cat, k_cat, v_exp, bp, tp, n_heads_a)

        wuk3 = jnp.transpose(w_uk[l], (1, 0, 2)).astype(BF16)
        wukt = jnp.transpose(w_uk[l], (1, 2, 0)).reshape(n_heads_a * nope, lat).astype(BF16)
        wuv3 = jnp.transpose(w_uv[l], (1, 0, 2)).astype(BF16)
        ob_s = _sample_attn(q_cat, c_bf, kr_bf, mp, cache_t, l, page_table, wuk3, wukt, wuv3, gkn,
                            ts, n_heads_a, rope)

        o_a = jnp.concatenate([oa_p, oa_s], axis=0)
        o_b = jnp.concatenate([ob_p, ob_s], axis=0)
        x = _out_proj(x, o_a, o_b, w_out[l].astype(BF16))

        i = l // 2
        if l % 2 == 0:
            x = _ffn_dense(x, row2(g_ffn[l]), w_ffn_gate[i].astype(BF16), w_ffn_up[i].astype(BF16),
                           w_ffn_down[i].astype(BF16))
        else:
            last = l == depth - 1
            parts = _moe(x, row2(g_ffn[l]), w_router[i], w_exp_gate[i].astype(BF16), w_exp_up[i].astype(BF16),
                         w_exp_down[i].astype(BF16), [(0, mp), (mp, mp + ms)] if last else [(0, mp + ms)])
            x = None if last else parts[0]

        rows_p.append(rows[:mp].reshape(bp, tp, lat + rope))
        rows_s.append(rows[mp:].reshape(bs, ts, lat + rope))
        wkv_p.append(wp)
        wkv_s.append(ws)
        last_p = p[:mp].reshape(bp, tp, -1)[:, -1, :rw]
        last_s = p[mp:].reshape(bs, ts, -1)[:, -1, :rw]
        sh_p.append(_rwkv_cols_inv(last_p, rd, wl, al, gl))
        sh_s.append(_rwkv_cols_inv(last_s, rd, wl, al, gl))

    if x is not None:
        parts = [x[:mp], x[mp:]]
    return (parts[0].reshape(bp, tp, d), parts[1].reshape(bs, ts, d),
            jnp.stack(rows_p), jnp.stack(rows_s), jnp.stack(wkv_p), jnp.stack(wkv_s),
            jnp.stack(sh_p), jnp.stack(sh_s))
```

```python
import functools
import math

import jax
import jax.numpy as jnp
from jax import lax
from jax.experimental import pallas as pl
from jax.experimental.pallas import tpu as pltpu

F32 = jnp.float32
BF16 = jnp.bfloat16
RMS_EPS = 1e-6
GN_EPS = 64e-5
ROPE_THETA = 10000.0
TOP_K = 2
PAGE_SIZE = 128
LANES = 128
VMEM_LIMIT = 56 * 1024 * 1024


def _pick(n, prefs):
    for p in prefs:
        if n % p == 0:
            return p
    raise ValueError(f"no tile for {n} in {prefs}")


def _round_up(n, m):
    return (n + m - 1) // m * m


def _params(*sem):
    return pltpu.CompilerParams(dimension_semantics=sem, vmem_limit_bytes=VMEM_LIMIT)


def _dot(a, b, **kw):
    return jnp.dot(a, b, preferred_element_type=F32, **kw)


def _dot_nt(a, b):
    return lax.dot_general(a, b, (((1,), (1,)), ((), ())), preferred_element_type=F32)


def _dot_tn(a, b):
    return lax.dot_general(a, b, (((0,), (0,)), ((), ())), preferred_element_type=F32)


def _rms(x, g, n=None):
    n = x.shape[-1] if n is None else n
    ms = jnp.sum(x * x, axis=-1, keepdims=True) * (1.0 / n)
    return x * lax.rsqrt(ms + RMS_EPS) * g


def _in_proj_kernel(x_ref, g_ref, w_ref, o_ref, hn_ref):
    @pl.when(pl.program_id(1) == 0)
    def _():
        hn_ref[...] = _rms(x_ref[...], g_ref[...]).astype(BF16)

    o_ref[...] = _dot(hn_ref[...], w_ref[...])


def _in_proj(x, g, w):
    m, d = x.shape
    n = w.shape[1]
    tm = _pick(m, (1024, 512, 256, 128, 64, 32, 16, 8))
    tn = _pick(n, (512, 256, 128))
    return pl.pallas_call(
        _in_proj_kernel,
        grid=(m // tm, n // tn),
        in_specs=[pl.BlockSpec((tm, d), lambda i, j: (i, 0)),
                  pl.BlockSpec((1, d), lambda i, j: (0, 0)),
                  pl.BlockSpec((d, tn), lambda i, j: (0, j))],
        out_specs=pl.BlockSpec((tm, tn), lambda i, j: (i, j)),
        out_shape=jax.ShapeDtypeStruct((m, n), F32),
        scratch_shapes=[pltpu.VMEM((tm, d), BF16)],
        compiler_params=_params("parallel", "arbitrary"),
        name="in_proj",
    )(x, g, w)


def _rwkv_kernel(*refs, n_heads, head, chunk, has_vres, head_group):
    it = iter(refs)
    p_ref, shift0_ref, s0_ref = next(it), next(it), next(it)
    vfirst_ref = next(it) if has_vres else None
    mu_ref, w0_ref, wdec_ref, a0_ref, wiclr_ref, wgate_ref = (next(it) for _ in range(6))
    kk_ref, ka_ref, rk_ref, lnw_ref, lnb_ref = (next(it) for _ in range(5))
    if has_vres:
        v0_ref, vra_ref, vrb_ref = next(it), next(it), next(it)
    o_ref = next(it)
    vout_ref = None if has_vres else next(it)
    slast_ref = next(it)
    s_scr, carry_scr, y_scr = next(it), next(it), next(it)

    c_idx = pl.program_id(1)
    rd = n_heads * head
    rw = mu_ref.shape[1]
    C = chunk

    @pl.when(c_idx == 0)
    def _():
        s_scr[...] = s0_ref[0]
        carry_scr[...] = shift0_ref[0]

    p = p_ref[:, :rw]
    rid = lax.broadcasted_iota(jnp.int32, p.shape, 0)
    prev = jnp.where(rid == 0, carry_scr[...], pltpu.roll(p, 1, 0))
    carry_scr[...] = p[C - 1:C, :]
    m = p + (prev - p) * mu_ref[...]

    r = m[:, :rd]
    k = m[:, rd:2 * rd]
    v = m[:, 2 * rd:3 * rd]
    pw = m[:, 3 * rd:3 * rd + LANES]
    pa = m[:, 3 * rd + LANES:3 * rd + 2 * LANES]
    pg = m[:, 3 * rd + 2 * LANES:]

    u = w0_ref[...] + _dot(jnp.tanh(pw).astype(BF16), wdec_ref[...])
    lw = (-math.exp(-0.5)) * jax.nn.sigmoid(u)
    a = jax.nn.sigmoid(a0_ref[...] + _dot(pa.astype(BF16), wiclr_ref[...]))
    g = _dot(jax.nn.sigmoid(pg).astype(BF16), wgate_ref[...])
    if has_vres:
        lo = _dot(v.astype(BF16), vra_ref[...])
        v = v + (vfirst_ref[...] - v) * jax.nn.sigmoid(v0_ref[...] + _dot(lo.astype(BF16), vrb_ref[...]))
    else:
        vout_ref[...] = v
    kk_raw = k * kk_ref[...]
    k = k * (1.0 + (a - 1.0) * ka_ref[...])

    row = lax.broadcasted_iota(jnp.int32, (C, C), 0)
    col = lax.broadcasted_iota(jnp.int32, (C, C), 1)
    incl = col <= row
    strict = col < row
    cum = _dot(incl.astype(F32), lw, precision=lax.Precision.HIGHEST)
    dec = jnp.exp(cum)
    dec_prev = jnp.exp(cum - lw)
    dec_inv = jnp.exp(-cum)

    eye = (col == row).astype(F32)
    mask2 = jnp.concatenate([strict, incl], axis=0)

    for h0 in range(0, n_heads, head_group):
        hs = range(h0, min(h0 + head_group, n_heads))
        sls = {h: slice(h * head, (h + 1) * head) for h in hs}
        ar, bk, vb, s_old = {}, {}, {}, {}
        for h in hs:
            sl = sls[h]
            kk = kk_raw[:, sl]
            kk = kk / jnp.maximum(jnp.sqrt(jnp.sum(kk * kk, axis=-1, keepdims=True)), 1e-12)
            a_t = (-kk) * dec_prev[:, sl]
            r_t = r[:, sl] * dec[:, sl]
            b_t = (kk * a[:, sl]) * dec_inv[:, sl]
            k_t = k[:, sl] * dec_inv[:, sl]
            ar[h] = jnp.concatenate([a_t, r_t], axis=0).astype(BF16)
            bk[h] = (b_t.astype(BF16), k_t.astype(BF16))
            vb[h] = v[:, sl].astype(BF16)
            s_old[h] = s_scr[h]
        g_b = {h: jnp.where(mask2, _dot_nt(ar[h], bk[h][0]), 0.0) for h in hs}
        g_k = {h: jnp.where(mask2, _dot_nt(ar[h], bk[h][1]), 0.0).astype(BF16) for h in hs}
        pw_l = {h: g_b[h][:C] for h in hs}
        inv = {h: eye + pw_l[h] for h in hs}
        n = 1
        while 2 * n < C:
            pw_b = {h: pw_l[h].astype(BF16) for h in hs}
            pw_l = {h: _dot(pw_b[h], pw_b[h]) for h in hs}
            inv = {h: inv[h] + _dot(inv[h].astype(BF16), pw_l[h].astype(BF16)) for h in hs}
            n *= 2
        xy = {h: _dot_nt(ar[h], s_old[h].astype(BF16)) + _dot(g_k[h], vb[h]) for h in hs}
        u_b = {h: _dot(inv[h].astype(BF16), xy[h][:C].astype(BF16)).astype(BF16) for h in hs}
        y = {h: xy[h][C:] + _dot(g_b[h][C:].astype(BF16), u_b[h]) for h in hs}
        for h in hs:
            sl = sls[h]
            upd = _dot_tn(jnp.concatenate([u_b[h], vb[h]], axis=0),
                          jnp.concatenate(bk[h], axis=0))
            s_scr[h] = (s_old[h] + upd) * dec[C - 1:C, sl]
            mean = jnp.mean(y[h], axis=-1, keepdims=True)
            yc = y[h] - mean
            var = jnp.mean(yc * yc, axis=-1, keepdims=True)
            yn = yc * lax.rsqrt(var + GN_EPS) * lnw_ref[:, sl] + lnb_ref[:, sl]
            bonus = jnp.sum(r[:, sl] * k[:, sl] * rk_ref[:, sl], axis=-1, keepdims=True) * v[:, sl]
            y_scr[:, sl] = yn + bonus

    o_ref[...] = (y_scr[...] * g).astype(o_ref.dtype)

    @pl.when(c_idx == pl.num_programs(1) - 1)
    def _():
        slast_ref[0] = s_scr[...]


def _rwkv_mixer(p, row0, bsz, t, shift0, s0, vfirst, prm, chunk):
    n_heads, head = s0.shape[1], s0.shape[2]
    rd = n_heads * head
    rw = prm["mu"].shape[1]
    has_vres = vfirst is not None
    nc = t // chunk
    blk0 = row0 // chunk
    np_ = p.shape[1]

    def full(a):
        return pl.BlockSpec(a.shape, lambda b, c: (0,) * a.ndim)

    ins = [p, shift0, s0]
    specs = [pl.BlockSpec((chunk, np_), lambda b, c: (blk0 + b * nc + c, 0)),
             pl.BlockSpec((1, 1, rw), lambda b, c: (b, 0, 0)),
             pl.BlockSpec((1, n_heads, head, head), lambda b, c: (b, 0, 0, 0))]
    if has_vres:
        ins.append(vfirst)
        specs.append(pl.BlockSpec((chunk, rd), lambda b, c: (b * nc + c, 0)))
    names = ["mu", "w0", "wdec", "a0", "wiclr", "wgate", "k_k", "k_a", "r_k", "lnx_w", "lnx_b"]
    if has_vres:
        names += ["v0", "vra", "vrb"]
    for nm in names:
        ins.append(prm[nm])
        specs.append(full(prm[nm]))

    row_spec = pl.BlockSpec((chunk, rd), lambda b, c: (b * nc + c, 0))
    outs = [jax.ShapeDtypeStruct((bsz * t, rd), BF16)]
    out_specs = [row_spec]
    if not has_vres:
        outs.append(jax.ShapeDtypeStruct((bsz * t, rd), F32))
        out_specs.append(row_spec)
    outs.append(jax.ShapeDtypeStruct(s0.shape, F32))
    out_specs.append(pl.BlockSpec((1, n_heads, head, head), lambda b, c: (b, 0, 0, 0)))

    res = pl.pallas_call(
        functools.partial(_rwkv_kernel, n_heads=n_heads, head=head, chunk=chunk, has_vres=has_vres,
                          head_group=n_heads),
        grid=(bsz, nc),
        in_specs=specs,
        out_specs=out_specs,
        out_shape=outs,
        scratch_shapes=[pltpu.VMEM((n_heads, head, head), F32),
                        pltpu.VMEM((1, rw), F32),
                        pltpu.VMEM((chunk, rd), F32)],
        compiler_params=_params("parallel", "arbitrary"),
        name="rwkv_mixer",
    )(*ins)
    if has_vres:
        o, s_last = res
        return o, vfirst, s_last
    o, v_out, s_last = res
    return o, v_out, s_last


def _rotate(x, cos, sin_a, sin_b):
    half = LANES // 4
    return x * cos + pltpu.roll(x, LANES - half, 1) * sin_a + pltpu.roll(x, half, 1) * sin_b


def _mla_proj_kernel(qa_ref, kv_ref, cos_ref, sina_ref, sinb_ref, gqa_ref, wqb_ref, gqn_ref, gqr_ref,
                     gkva_ref, gkr_ref, q_ref, rows_ref, c_ref, kr_ref, *, n_heads, rope, scale):
    cos, sin_a, sin_b = cos_ref[...], sina_ref[...], sinb_ref[...]
    hq = _rms(qa_ref[...], gqa_ref[...]).astype(BF16)
    q = _dot(hq, wqb_ref[...])
    hw = n_heads * LANES
    for h in range(n_heads):
        qn = _rms(q[:, h * LANES:(h + 1) * LANES], gqn_ref[...]) * scale
        qr = _rms(q[:, hw + h * LANES:hw + (h + 1) * LANES], gqr_ref[...], n=rope)
        qr = _rotate(qr, cos, sin_a, sin_b) * scale
        q_ref[:, 2 * h * LANES:(2 * h + 1) * LANES] = qn.astype(q_ref.dtype)
        q_ref[:, (2 * h + 1) * LANES:(2 * h + 2) * LANES] = qr.astype(q_ref.dtype)
    kv = kv_ref[...]
    lat = gkva_ref.shape[1]
    c = _rms(kv[:, :lat], gkva_ref[...])
    kr = _rotate(_rms(kv[:, lat:lat + LANES], gkr_ref[...], n=rope), cos, sin_a, sin_b)
    rows_ref[:, :lat] = c
    rows_ref[:, lat:] = kr[:, :rope]
    c_ref[...] = c.astype(c_ref.dtype)
    kr_ref[...] = kr.astype(kr_ref.dtype)


def _mla_proj(p, col_q, col_kv, tabs, prm, n_heads, rope, scale):
    m = p.shape[0]
    tm = _pick(m, (512, 256, 128, 64, 32, 16, 8))
    qw = prm["g_qa"].shape[1]
    lat = prm["g_kva"].shape[1]
    assert qw == 512 and col_q % qw == 0 and col_kv % qw == 0

    def full(a):
        return pl.BlockSpec(a.shape, lambda i: (0,) * a.ndim)

    tab_spec = pl.BlockSpec((tm, LANES), lambda i: (i, 0))
    ws = [prm["g_qa"], prm["w_qb"], prm["g_qn"], prm["g_qr"], prm["g_kva"], prm["g_kr"]]
    return pl.pallas_call(
        functools.partial(_mla_proj_kernel, n_heads=n_heads, rope=rope, scale=scale),
        grid=(m // tm,),
        in_specs=[pl.BlockSpec((tm, qw), lambda i: (i, col_q // qw)),
                  pl.BlockSpec((tm, qw), lambda i: (i, col_kv // qw)),
                  tab_spec, tab_spec, tab_spec] + [full(a) for a in ws],
        out_specs=[pl.BlockSpec((tm, 2 * n_heads * LANES), lambda i: (i, 0)),
                   pl.BlockSpec((tm, lat + rope), lambda i: (i, 0)),
                   pl.BlockSpec((tm, lat), lambda i: (i, 0)),
                   pl.BlockSpec((tm, LANES), lambda i: (i, 0))],
        out_shape=[jax.ShapeDtypeStruct((m, 2 * n_heads * LANES), BF16),
                   jax.ShapeDtypeStruct((m, lat + rope), F32),
                   jax.ShapeDtypeStruct((m, lat), BF16),
                   jax.ShapeDtypeStruct((m, LANES), BF16)],
        compiler_params=_params("parallel"),
        name="mla_proj",
    )(p, p, *tabs, *ws)


def _kv_expand_kernel(c_ref, kr_ref, wuk_ref, wuv_ref, gkn_ref, k_ref, v_ref, *, n_heads):
    c = c_ref[...]
    ke = _dot(c, wuk_ref[...])
    kr = kr_ref[...]
    for h in range(n_heads):
        kh = _rms(ke[:, h * LANES:(h + 1) * LANES], gkn_ref[...])
        k_ref[:, 2 * h * LANES:(2 * h + 1) * LANES] = kh.astype(k_ref.dtype)
        k_ref[:, (2 * h + 1) * LANES:(2 * h + 2) * LANES] = kr
    v_ref[...] = _dot_nt(wuv_ref[...], c).astype(v_ref.dtype)


def _kv_expand(c, kr, mp, wuk, wuv_t, gkn, n_heads):
    lat = c.shape[1]
    tm = _pick(mp, (512, 256, 128))
    assert wuk.shape[1] == n_heads * LANES and wuv_t.shape[0] == n_heads * LANES

    def full(a):
        return pl.BlockSpec(a.shape, lambda i: (0,) * a.ndim)

    return pl.pallas_call(
        functools.partial(_kv_expand_kernel, n_heads=n_heads),
        grid=(mp // tm,),
        in_specs=[pl.BlockSpec((tm, lat), lambda i: (i, 0)),
                  pl.BlockSpec((tm, LANES), lambda i: (i, 0)),
                  full(wuk), full(wuv_t), full(gkn)],
        out_specs=[pl.BlockSpec((tm, 2 * n_heads * LANES), lambda i: (i, 0)),
                   pl.BlockSpec((n_heads * LANES, tm), lambda i: (0, i))],
        out_shape=[jax.ShapeDtypeStruct((mp, 2 * n_heads * LANES), BF16),
                   jax.ShapeDtypeStruct((n_heads * LANES, mp), BF16)],
        compiler_params=_params("parallel"),
        name="kv_expand",
    )(c, kr, wuk, wuv_t, gkn)


def _flash_kernel(qi_ref, ki_ref, q_ref, k_ref, v_ref, o_ref, m_scr, l_scr, acc_scr, *, tq, tk):
    pair = pl.program_id(2)
    qi, ki = qi_ref[pair], ki_ref[pair]

    @pl.when(ki == 0)
    def _():
        m_scr[...] = jnp.full(m_scr.shape, -jnp.inf, F32)
        l_scr[...] = jnp.zeros(l_scr.shape, F32)
        acc_scr[...] = jnp.zeros(acc_scr.shape, F32)

    s = _dot_nt(k_ref[...], q_ref[...])
    key = ki * tk + lax.broadcasted_iota(jnp.int32, s.shape, 0)
    qry = qi * tq + lax.broadcasted_iota(jnp.int32, s.shape, 1)
    s = jnp.where(key <= qry, s, -jnp.inf)
    m_old = m_scr[...]
    m_new = jnp.maximum(m_old, jnp.max(s, axis=0, keepdims=True))
    alpha = jnp.exp(m_old - m_new)
    p = jnp.exp(s - m_new)
    l_scr[...] = alpha * l_scr[...] + jnp.sum(p, axis=0, keepdims=True)
    acc_scr[...] = alpha * acc_scr[...] + _dot(v_ref[...], p.astype(BF16))
    m_scr[...] = m_new

    @pl.when((ki + 1) * tk >= (qi + 1) * tq)
    def _():
        o_ref[...] = (acc_scr[...] / l_scr[...]).T.astype(o_ref.dtype)


def _flash_prompt(q, k, v, bsz, t, n_heads):
    tk = _pick(t, (512, 256, 128))
    tq = 2 * tk if t % (2 * tk) == 0 else tk
    nq, nk = t // tq, t // tk
    pairs = [(qi, ki) for qi in range(nq) for ki in range((qi + 1) * tq // tk)]
    qi_arr = jnp.asarray([pr[0] for pr in pairs], jnp.int32)
    ki_arr = jnp.asarray([pr[1] for pr in pairs], jnp.int32)
    grid_spec = pltpu.PrefetchScalarGridSpec(
        num_scalar_prefetch=2,
        grid=(bsz, n_heads, len(pairs)),
        in_specs=[pl.BlockSpec((tq, 2 * LANES), lambda b, h, pr, qa, ka: (b * nq + qa[pr], h)),
                  pl.BlockSpec((tk, 2 * LANES), lambda b, h, pr, qa, ka: (b * nk + ka[pr], h)),
                  pl.BlockSpec((LANES, tk), lambda b, h, pr, qa, ka: (h, b * nk + ka[pr]))],
        out_specs=pl.BlockSpec((tq, LANES), lambda b, h, pr, qa, ka: (b * nq + qa[pr], h)),
        scratch_shapes=[pltpu.VMEM((1, tq), F32), pltpu.VMEM((1, tq), F32), pltpu.VMEM((LANES, tq), F32)],
    )
    return pl.pallas_call(
        functools.partial(_flash_kernel, tq=tq, tk=tk),
        grid_spec=grid_spec,
        out_shape=jax.ShapeDtypeStruct((bsz * t, n_heads * LANES), BF16),
        compiler_params=_params("parallel", "parallel", "arbitrary"),
        name="flash_prompt",
    )(qi_arr, ki_arr, q, k, v)


def _sample_attn_kernel(pt_ref, q_ref, cn_ref, krn_ref, wuk3_ref, wukt_ref, wuv3_ref, gkn_ref, *rest,
                        n_heads, n_tok, n_page_in, sub_pages, lat, rope):
    page_refs = rest[:n_page_in]
    o_ref = rest[n_page_in]
    lhs_scr, qr_scr, m_scr, l_scr, acc_scr = rest[n_page_in + 1:]
    j = pl.program_id(1)
    nq = n_heads * n_tok
    hk = n_heads * LANES

    @pl.when(j == 0)
    def _():
        lhs_scr[:hk, :] = wukt_ref[...]
        for h in range(n_heads):
            qn = (q_ref[:, 2 * h * LANES:(2 * h + 1) * LANES].astype(F32) * gkn_ref[...]).astype(BF16)
            lhs_scr[hk + h * n_tok:hk + (h + 1) * n_tok, :] = _dot_nt(qn, wuk3_ref[h]).astype(BF16)
            qr_scr[h * n_tok:(h + 1) * n_tok, :] = q_ref[:, (2 * h + 1) * LANES:(2 * h + 2) * LANES]
        m_scr[...] = jnp.full(m_scr.shape, -jnp.inf, F32)
        l_scr[...] = jnp.zeros(l_scr.shape, F32)
        acc_scr[...] = jnp.zeros(acc_scr.shape, F32)

    def scores(big, s_rope):
        npos = big.shape[1]
        invs = []
        for h in range(n_heads):
            kh = big[h * LANES:(h + 1) * LANES]
            ms = jnp.sum(kh * kh, axis=0, keepdims=True) * (1.0 / LANES)
            invs.append(jnp.broadcast_to(lax.rsqrt(ms + RMS_EPS), (n_tok, npos)))
        return big[hk:] * jnp.concatenate(invs, axis=0) + s_rope

    def update(s, pv):
        m_old = m_scr[...]
        m_new = jnp.maximum(m_old, jnp.max(s, axis=-1, keepdims=True))
        alpha = jnp.exp(m_old - m_new)
        p = jnp.exp(s - m_new)
        l_scr[...] = alpha * l_scr[...] + jnp.sum(p, axis=-1, keepdims=True)
        acc_scr[...] = alpha * acc_scr[...] + pv(p.astype(BF16))
        m_scr[...] = m_new

    def load_tile(sub):
        pages = [page_refs[sub * sub_pages + i][0, 0] for i in range(sub_pages)]
        c_t = jnp.concatenate([pg[:lat].astype(BF16) for pg in pages], axis=1)
        kr_t = jnp.concatenate([pg[lat:lat + rope].astype(BF16) for pg in pages], axis=1)
        return c_t, _dot(lhs_scr[...], c_t), _dot(qr_scr[:, :rope], kr_t)

    n_sub = n_page_in // sub_pages
    parts = []
    nxt = load_tile(0)
    for sub in range(n_sub):
        c_t, big, s_rope = nxt
        if sub + 1 < n_sub:
            nxt = load_tile(sub + 1)
        s = scores(big, s_rope)
        m_i = jnp.max(s, axis=-1, keepdims=True)
        p = jnp.exp(s - m_i)
        parts.append((m_i, jnp.sum(p, axis=-1, keepdims=True), _dot_nt(p.astype(BF16), c_t)))
    m_old = m_scr[...]
    m_new = m_old
    for m_i, _, _ in parts:
        m_new = jnp.maximum(m_new, m_i)
    alpha = jnp.exp(m_old - m_new)
    l_new = alpha * l_scr[...]
    acc_new = alpha * acc_scr[...]
    for m_i, l_i, acc_i in parts:
        w_i = jnp.exp(m_i - m_new)
        l_new = l_new + w_i * l_i
        acc_new = acc_new + w_i * acc_i
    l_scr[...] = l_new
    acc_scr[...] = acc_new
    m_scr[...] = m_new

    @pl.when(j == pl.num_programs(1) - 1)
    def _():
        c_n = cn_ref[...]
        row = lax.broadcasted_iota(jnp.int32, (nq, n_tok), 0)
        col = lax.broadcasted_iota(jnp.int32, (nq, n_tok), 1)
        s = scores(_dot_nt(lhs_scr[...], c_n), _dot_nt(qr_scr[:, :rope], krn_ref[:, :rope]))
        update(jnp.where(col <= (row % n_tok), s, -jnp.inf), lambda p: _dot(p, c_n))
        o_lat = (acc_scr[...] / l_scr[...]).astype(BF16)
        for h in range(n_heads):
            o_h = _dot(o_lat[h * n_tok:(h + 1) * n_tok], wuv3_ref[h])
            o_ref[:, h * LANES:(h + 1) * LANES] = o_h.astype(o_ref.dtype)


def _sample_attn(q, c_new, kr_new, row0, cache_t, layer, page_table, wuk3, wukt, wuv3, gkn,
                 n_tok, n_heads, rope):
    bs, n_pages = page_table.shape
    lat = c_new.shape[1]
    kvw, page = cache_t.shape[2], cache_t.shape[3]
    n_page_in = _pick(n_pages, (16, 8, 4, 2, 1))
    sub_pages = _pick(n_page_in, (4, 2, 1))
    steps = n_pages // n_page_in
    blk0 = row0 // n_tok
    nq = n_heads * n_tok

    def full(a):
        return pl.BlockSpec(a.shape, lambda b, j, pt: (0,) * a.ndim)

    def page_spec(i):
        return pl.BlockSpec((1, 1, kvw, page),
                            lambda b, j, pt: (pt[b * n_pages + j * n_page_in + i], layer, 0, 0))

    grid_spec = pltpu.PrefetchScalarGridSpec(
        num_scalar_prefetch=1,
        grid=(bs, steps),
        in_specs=[pl.BlockSpec((n_tok, 2 * n_heads * LANES), lambda b, j, pt: (blk0 + b, 0)),
                  pl.BlockSpec((n_tok, lat), lambda b, j, pt: (blk0 + b, 0)),
                  pl.BlockSpec((n_tok, LANES), lambda b, j, pt: (blk0 + b, 0)),
                  full(wuk3), full(wukt), full(wuv3), full(gkn)] + [page_spec(i) for i in range(n_page_in)],
        out_specs=pl.BlockSpec((n_tok, n_heads * LANES), lambda b, j, pt: (b, 0)),
        scratch_shapes=[pltpu.VMEM((n_heads * LANES + nq, lat), BF16), pltpu.VMEM((nq, LANES), BF16),
                        pltpu.VMEM((nq, 1), F32), pltpu.VMEM((nq, 1), F32), pltpu.VMEM((nq, lat), F32)],
    )
    return pl.pallas_call(
        functools.partial(_sample_attn_kernel, n_heads=n_heads, n_tok=n_tok, n_page_in=n_page_in,
                          sub_pages=sub_pages, lat=lat, rope=rope),
        grid_spec=grid_spec,
        out_shape=jax.ShapeDtypeStruct((bs * n_tok, n_heads * LANES), BF16),
        compiler_params=_params("parallel", "arbitrary"),
        name="sample_attn",
    )(page_table.reshape(-1), q, c_new, kr_new, wuk3, wukt, wuv3, gkn, *([cache_t] * n_page_in))


def _out_proj_kernel(x_ref, a_ref, b_ref, w_ref, o_ref):
    ka = a_ref.shape[1]
    o_ref[...] = x_ref[...] + _dot(a_ref[...], w_ref[:ka, :]) + _dot(b_ref[...], w_ref[ka:, :])


def _out_proj(x, a, b, w):
    m, d = x.shape
    tm = _pick(m, (1024, 512, 256, 128, 64, 32, 16, 8))
    tn = _pick(d, (512, 256, 128))
    return pl.pallas_call(
        _out_proj_kernel,
        grid=(m // tm, d // tn),
        in_specs=[pl.BlockSpec((tm, tn), lambda i, j: (i, j)),
                  pl.BlockSpec((tm, a.shape[1]), lambda i, j: (i, 0)),
                  pl.BlockSpec((tm, b.shape[1]), lambda i, j: (i, 0)),
                  pl.BlockSpec((w.shape[0], tn), lambda i, j: (0, j))],
        out_specs=pl.BlockSpec((tm, tn), lambda i, j: (i, j)),
        out_shape=jax.ShapeDtypeStruct((m, d), F32),
        compiler_params=_params("parallel", "parallel"),
        name="out_proj",
    )(x, a, b, w)


def _ffn_kernel(x_ref, g_ref, wg_ref, wu_ref, wd_ref, o_ref, hn_ref):
    @pl.when(pl.program_id(1) == 0)
    def _():
        x = x_ref[...]
        hn_ref[...] = _rms(x, g_ref[...]).astype(BF16)
        o_ref[...] = x

    h = hn_ref[...]
    gt = _dot(h, wg_ref[...])
    act = (gt * jax.nn.sigmoid(gt) * _dot(h, wu_ref[...])).astype(BF16)
    o_ref[...] += _dot(act, wd_ref[...])


def _ffn_dense(x, g, wg, wu, wd):
    m, d = x.shape
    f = wg.shape[1]
    tm = _pick(m, (512, 256, 128, 64, 32, 16, 8))
    tf = _pick(f, (512, 256, 128))
    return pl.pallas_call(
        _ffn_kernel,
        grid=(m // tm, f // tf),
        in_specs=[pl.BlockSpec((tm, d), lambda i, j: (i, 0)),
                  pl.BlockSpec((1, d), lambda i, j: (0, 0)),
                  pl.BlockSpec((d, tf), lambda i, j: (0, j)),
                  pl.BlockSpec((d, tf), lambda i, j: (0, j)),
                  pl.BlockSpec((tf, d), lambda i, j: (j, 0))],
        out_specs=pl.BlockSpec((tm, d), lambda i, j: (i, 0)),
        out_shape=jax.ShapeDtypeStruct((m, d), F32),
        scratch_shapes=[pltpu.VMEM((tm, d), BF16)],
        compiler_params=_params("parallel", "arbitrary"),
        name="ffn_dense",
    )(x, g, wg, wu, wd)


def _router_kernel(x_ref, g_ref, wr_ref, hn_ref, lg_ref):
    hn = _rms(x_ref[...], g_ref[...])
    hn_ref[...] = hn.astype(hn_ref.dtype)
    lg_ref[...] = _dot(hn, wr_ref[...], precision=lax.Precision.HIGHEST)


def _router(x, g, wr):
    m, d = x.shape
    tm = _pick(m, (512, 256, 128, 64, 32, 16, 8))
    return pl.pallas_call(
        _router_kernel,
        grid=(m // tm,),
        in_specs=[pl.BlockSpec((tm, d), lambda i: (i, 0)),
                  pl.BlockSpec((1, d), lambda i: (0, 0)),
                  pl.BlockSpec(wr.shape, lambda i: (0, 0))],
        out_specs=[pl.BlockSpec((tm, d), lambda i: (i, 0)),
                   pl.BlockSpec((tm, wr.shape[1]), lambda i: (i, 0))],
        out_shape=[jax.ShapeDtypeStruct((m, d), BF16),
                   jax.ShapeDtypeStruct((m, wr.shape[1]), F32)],
        compiler_params=_params("parallel"),
        name="router",
    )(x, g, wr)


def _experts_kernel(blk_e_ref, n_used_ref, xs_ref, gate_ref, wg_ref, wu_ref, wd_ref, o_ref):
    i, j = pl.program_id(0), pl.program_id(1)

    @pl.when(j == 0)
    def _():
        o_ref[...] = jnp.zeros(o_ref.shape, o_ref.dtype)

    @pl.when(i < n_used_ref[0])
    def _():
        x = xs_ref[...]
        gt = _dot(x, wg_ref[0].astype(BF16))
        act = (gt * jax.nn.sigmoid(gt) * _dot(x, wu_ref[0].astype(BF16))).astype(BF16)
        o_ref[...] += _dot(act, wd_ref[0].astype(BF16))

    @pl.when(j == pl.num_programs(1) - 1)
    def _():
        o_ref[...] = o_ref[...] * gate_ref[...]


def _experts(xs, row_gate, blk_e, n_used, wg, wu, wd, bm):
    n_rows, d = xs.shape
    f = wg.shape[2]
    tf = _pick(f, (256, 128))
    grid_spec = pltpu.PrefetchScalarGridSpec(
        num_scalar_prefetch=2,
        grid=(n_rows // bm, f // tf),
        in_specs=[pl.BlockSpec((bm, d), lambda i, j, be, nu: (i, 0)),
                  pl.BlockSpec((bm, 1), lambda i, j, be, nu: (i, 0)),
                  pl.BlockSpec((1, d, tf), lambda i, j, be, nu: (be[i], 0, j)),
                  pl.BlockSpec((1, d, tf), lambda i, j, be, nu: (be[i], 0, j)),
                  pl.BlockSpec((1, tf, d), lambda i, j, be, nu: (be[i], j, 0))],
        out_specs=pl.BlockSpec((bm, d), lambda i, j, be, nu: (i, 0)),
    )
    return pl.pallas_call(
        _experts_kernel,
        grid_spec=grid_spec,
        out_shape=jax.ShapeDtypeStruct((n_rows, d), F32),
        compiler_params=_params("parallel", "arbitrary"),
        name="experts",
    )(blk_e, n_used, xs, row_gate, wg, wu, wd)


def _moe(x, g, wr, wg, wu, wd, splits):
    m, d = x.shape
    n_exp = wg.shape[0]
    wr_pad = jnp.pad(wr, ((0, 0), (0, LANES - n_exp)))
    hn, logits = _router(x, g, wr_pad)
    top_logit, top_e = lax.top_k(logits[:, :n_exp], TOP_K)
    gates = jax.nn.softmax(top_logit, axis=-1)
    n_assign = m * TOP_K
    bm = _pick(m, (1024, 512, 256, 128, 64, 32, 16, 8))
    flat_e = top_e.reshape(-1)
    onehot = (flat_e[:, None] == jnp.arange(n_exp, dtype=flat_e.dtype)[None, :]).astype(jnp.int32)
    csum = jnp.cumsum(onehot, axis=0)
    counts = csum[-1]
    rank = jnp.sum((csum - onehot) * onehot, axis=1)
    padded = (counts + bm - 1) // bm * bm
    pad_end = jnp.cumsum(padded)
    pad_start = pad_end - padded
    slot = pad_start[flat_e] + rank
    n_rows = _round_up(n_assign + n_exp * (bm - 1), bm)
    n_blocks = n_rows // bm
    flat_tok = jnp.repeat(jnp.arange(m, dtype=jnp.int32), TOP_K)
    assert m < 2 ** 24
    fill = jnp.broadcast_to(jnp.asarray([m, 0.0], F32), (n_rows, 2))
    row_info = fill.at[slot].set(jnp.stack([flat_tok.astype(F32), gates.reshape(-1)], axis=1))
    row_tok = row_info[:, 0].astype(jnp.int32)
    row_gate = row_info[:, 1]
    blk_start = jnp.arange(n_blocks, dtype=jnp.int32) * bm
    blk_e = jnp.minimum(jnp.sum(pad_end[None, :] <= blk_start[:, None], axis=1), n_exp - 1).astype(jnp.int32)
    n_used = (pad_end[-1] // bm).astype(jnp.int32).reshape(1)
    hn_pad = jnp.concatenate([hn, jnp.zeros((1, d), hn.dtype)], axis=0)
    xs = hn_pad[row_tok]
    yb = _experts(xs, row_gate[:, None], blk_e, n_used, wg, wu, wd, bm)
    slot2 = slot.reshape(m, TOP_K)
    return [x[a:b] + yb[slot2[a:b, 0]] + yb[slot2[a:b, 1]] for a, b in splits]


def _pad_cols(a, width):
    return jnp.pad(a, [(0, 0)] * (a.ndim - 1) + [(0, width - a.shape[-1])])


def _rwkv_cols(a, rd, wl, al):
    o = 3 * rd
    return jnp.concatenate([a[..., :o], _pad_cols(a[..., o:o + wl], LANES),
                            _pad_cols(a[..., o + wl:o + wl + al], LANES),
                            _pad_cols(a[..., o + wl + al:], 2 * LANES)], axis=-1)


def _rwkv_cols_inv(a, rd, wl, al, gl):
    o = 3 * rd
    return jnp.concatenate([a[..., :o], a[..., o:o + wl], a[..., o + LANES:o + LANES + al],
                            a[..., o + 2 * LANES:o + 2 * LANES + gl]], axis=-1)


def _pad_rows(a, rows):
    return jnp.pad(a, ((0, rows - a.shape[0]), (0, 0)))


def _rope_tables(pos, rope):
    half = rope // 2
    assert half == LANES // 4
    inv_freq = ROPE_THETA ** (-jnp.arange(half, dtype=F32) * (2.0 / rope))
    ang = pos.astype(F32)[:, None] * inv_freq[None, :]
    cos, sin = jnp.cos(ang), jnp.sin(ang)
    z = jnp.zeros_like(cos)
    return (jnp.concatenate([cos, cos, z, z], axis=1),
            jnp.concatenate([-sin, z, z, z], axis=1),
            jnp.concatenate([z, sin, z, z], axis=1))


def kernel(x_prompt, x_sample, cache_kv, page_table, state_wkv, state_shift, g_mix, w_in, mu_shift, w0, w_decay, a0, w_iclr, w_gate_out, k_k, k_a, r_k, lnx_w, lnx_b, v0, v_res_a, v_res_b, g_qa, w_qb, g_qn, g_qr, g_kva, g_kr, w_uk, w_uv, g_kn, w_out, g_ffn, w_ffn_gate, w_ffn_up, w_ffn_down, w_router, w_exp_gate, w_exp_up, w_exp_down):
    bp, tp, d = x_prompt.shape
    bs, ts, _ = x_sample.shape
    depth = g_mix.shape[0]
    n_heads_r, head_r = r_k.shape[1], r_k.shape[2]
    rd = n_heads_r * head_r
    wl, al, gl = w_decay.shape[1], w_iclr.shape[1], w_gate_out.shape[1]
    rw_orig = 3 * rd + wl + al + gl
    ql, lat, rope, nope = g_qa.shape[1], g_kva.shape[1], g_kr.shape[1], g_qn.shape[1]
    n_heads_a, mla_v = w_uk.shape[2], w_uv.shape[3]
    assert nope == LANES and mla_v == LANES and rope == LANES // 2
    assert wl <= LANES and al <= LANES and gl <= 2 * LANES and v_res_a.shape[2] <= LANES
    past_len = page_table.shape[1] * PAGE_SIZE
    mp, ms = bp * tp, bs * ts
    scale = (nope + rope) ** -0.5
    rw = 3 * rd + 4 * LANES
    col_q, col_kv = rw, rw + ql
    chunk_p = _pick(tp, (64, 32, 16, 8))
    chunk_s = _pick(ts, (64, 32, 16, 8))

    x = jnp.concatenate([x_prompt.reshape(mp, d), x_sample.reshape(ms, d)], axis=0)
    pos = jnp.concatenate([jnp.tile(jnp.arange(tp, dtype=jnp.int32), bp),
                           jnp.tile(past_len + jnp.arange(ts, dtype=jnp.int32), bs)])
    tabs = _rope_tables(pos, rope)
    row2 = lambda a: a.reshape(1, -1)
    cache_t = jnp.transpose(cache_kv, (0, 1, 3, 2))

    shift0_p = jnp.zeros((bp, 1, rw), F32)
    wkv0_p = jnp.zeros((bp, n_heads_r, head_r, head_r), F32)
    vf_p = vf_s = None
    rows_p, rows_s, wkv_p, wkv_s, sh_p, sh_s = [], [], [], [], [], []
    for l in range(depth):
        w_in_l = w_in[l]
        w_pad = jnp.concatenate([
            _rwkv_cols(w_in_l[:, :rw_orig], rd, wl, al),
            w_in_l[:, rw_orig:rw_orig + ql + lat],
            _pad_cols(w_in_l[:, rw_orig + ql + lat:], 2 * LANES)], axis=1).astype(BF16)
        p = _in_proj(x, row2(g_mix[l]), w_pad)

        prm = {
            "mu": _rwkv_cols(row2(mu_shift[l]), rd, wl, al),
            "w0": row2(w0[l]), "a0": row2(a0[l]),
            "wdec": _pad_rows(w_decay[l], LANES).astype(BF16),
            "wiclr": _pad_rows(w_iclr[l], LANES).astype(BF16),
            "wgate": _pad_rows(w_gate_out[l], 2 * LANES).astype(BF16),
            "k_k": row2(k_k[l]), "k_a": row2(k_a[l]), "r_k": row2(r_k[l]),
            "lnx_w": row2(lnx_w[l]), "lnx_b": row2(lnx_b[l]),
        }
        if l > 0:
            prm["v0"] = row2(v0[l - 1])
            prm["vra"] = _pad_cols(v_res_a[l - 1], LANES).astype(BF16)
            prm["vrb"] = _pad_rows(v_res_b[l - 1], LANES).astype(BF16)
        oa_p, vf_p, wp = _rwkv_mixer(p, 0, bp, tp, shift0_p, wkv0_p, vf_p, prm, chunk_p)
        shift0_s = _rwkv_cols(state_shift[l], rd, wl, al)[:, None, :]
        oa_s, vf_s, ws = _rwkv_mixer(p, mp, bs, ts, shift0_s, state_wkv[l], vf_s, prm, chunk_s)

        wqb3 = w_qb[l].reshape(ql, n_heads_a, nope + rope)
        wqb_pad = jnp.concatenate([wqb3[:, :, :nope].reshape(ql, n_heads_a * nope),
                                   _pad_cols(wqb3[:, :, nope:], LANES).reshape(ql, n_heads_a * LANES)],
                                  axis=1).astype(BF16)
        mprm = {"g_qa": row2(g_qa[l]), "w_qb": wqb_pad, "g_qn": row2(g_qn[l]),
                "g_qr": _pad_cols(row2(g_qr[l]), LANES), "g_kva": row2(g_kva[l]),
                "g_kr": _pad_cols(row2(g_kr[l]), LANES)}
        q_cat, rows, c_bf, kr_bf = _mla_proj(p, col_q, col_kv, tabs, mprm, n_heads_a, rope, scale)

        wuk2 = w_uk[l].reshape(lat, n_heads_a * nope).astype(BF16)
        wuv_t = jnp.transpose(w_uv[l], (1, 2, 0)).reshape(n_heads_a * mla_v, lat).astype(BF16)
        gkn = row2(g_kn[l])
        k_cat, v_exp = _kv_expand(c_bf, kr_bf, mp, wuk2, wuv_t, gkn, n_heads_a)
        ob_p = _flash_prompt(q_cat, k_cat, v_exp, bp, tp, n_heads_a)

        wuk3 = jnp.transpose(w_uk[l], (1, 0, 2)).astype(BF16)
        wukt = jnp.transpose(w_uk[l], (1, 2, 0)).reshape(n_heads_a * nope, lat).astype(BF16)
        wuv3 = jnp.transpose(w_uv[l], (1, 0, 2)).astype(BF16)
        ob_s = _sample_attn(q_cat, c_bf, kr_bf, mp, cache_t, l, page_table, wuk3, wukt, wuv3, gkn,
                            ts, n_heads_a, rope)

        o_a = jnp.concatenate([oa_p, oa_s], axis=0)
        o_b = jnp.concatenate([ob_p, ob_s], axis=0)
        x = _out_proj(x, o_a, o_b, w_out[l].astype(BF16))

        i = l // 2
        if l % 2 == 0:
            x = _ffn_dense(x, row2(g_ffn[l]), w_ffn_gate[i].astype(BF16), w_ffn_up[i].astype(BF16),
                           w_ffn_down[i].astype(BF16))
        else:
            last = l == depth - 1
            parts = _moe(x, row2(g_ffn[l]), w_router[i], w_exp_gate[i], w_exp_up[i], w_exp_down[i],
                         [(0, mp), (mp, mp + ms)] if last else [(0, mp + ms)])
            x = None if last else parts[0]

        rows_p.append(rows[:mp].reshape(bp, tp, lat + rope))
        rows_s.append(rows[mp:].reshape(bs, ts, lat + rope))
        wkv_p.append(wp)
        wkv_s.append(ws)
        last_p = p[tp - 1:mp:tp, :rw]
        last_s = p[mp + ts - 1::ts, :rw]
        sh_p.append(_rwkv_cols_inv(last_p, rd, wl, al, gl))
        sh_s.append(_rwkv_cols_inv(last_s, rd, wl, al, gl))

    if x is not None:
        parts = [x[:mp], x[mp:]]
    return (parts[0].reshape(bp, tp, d), parts[1].reshape(bs, ts, d),
            jnp.stack(rows_p), jnp.stack(rows_s), jnp.stack(wkv_p), jnp.stack(wkv_s),
            jnp.stack(sh_p), jnp.stack(sh_s))
```

```python
import functools
import math

import jax
import jax.numpy as jnp
from jax import lax
from jax.experimental import pallas as pl
from jax.experimental.pallas import tpu as pltpu

F32 = jnp.float32
BF16 = jnp.bfloat16
RMS_EPS = 1e-6
GN_EPS = 64e-5
ROPE_THETA = 10000.0
TOP_K = 2
PAGE_SIZE = 128
LANES = 128
VMEM_LIMIT = 56 * 1024 * 1024
NEUTRAL_SCORE = -1e30


def _pick(n, prefs):
    for p in prefs:
        if n % p == 0:
            return p
    raise ValueError(f"no tile for {n} in {prefs}")


def _round_up(n, m):
    return (n + m - 1) // m * m


def _params(*sem):
    return pltpu.CompilerParams(dimension_semantics=sem, vmem_limit_bytes=VMEM_LIMIT)


def _dot(a, b, **kw):
    return jnp.dot(a, b, preferred_element_type=F32, **kw)


def _dot_nt(a, b):
    return lax.dot_general(a, b, (((1,), (1,)), ((), ())), preferred_element_type=F32)


def _dot_tn(a, b):
    return lax.dot_general(a, b, (((0,), (0,)), ((), ())), preferred_element_type=F32)


def _rms(x, g, n=None):
    n = x.shape[-1] if n is None else n
    ms = jnp.sum(x * x, axis=-1, keepdims=True) * (1.0 / n)
    return x * lax.rsqrt(ms + RMS_EPS) * g


def _in_proj_kernel(x_ref, g_ref, w_ref, o_ref, hn_ref):
    @pl.when(pl.program_id(1) == 0)
    def _():
        hn_ref[...] = _rms(x_ref[...], g_ref[...]).astype(BF16)

    o_ref[...] = _dot(hn_ref[...], w_ref[...])


def _in_proj(x, g, w):
    m, d = x.shape
    n = w.shape[1]
    tm = _pick(m, (1024, 512, 256, 128, 64, 32, 16, 8))
    tn = _pick(n, (1536, 512, 256, 128))
    return pl.pallas_call(
        _in_proj_kernel,
        grid=(m // tm, n // tn),
        in_specs=[pl.BlockSpec((tm, d), lambda i, j: (i, 0)),
                  pl.BlockSpec((1, d), lambda i, j: (0, 0)),
                  pl.BlockSpec((d, tn), lambda i, j: (0, j))],
        out_specs=pl.BlockSpec((tm, tn), lambda i, j: (i, j)),
        out_shape=jax.ShapeDtypeStruct((m, n), F32),
        scratch_shapes=[pltpu.VMEM((tm, d), BF16)],
        compiler_params=_params("parallel", "arbitrary"),
        name="in_proj",
    )(x, g, w)


def _rwkv_kernel(*refs, n_heads, head, chunk, has_vres, head_group):
    it = iter(refs)
    p_ref, shift0_ref, s0_ref = next(it), next(it), next(it)
    vfirst_ref = next(it) if has_vres else None
    mu_ref, w0_ref, wdec_ref, a0_ref, wiclr_ref, wgate_ref = (next(it) for _ in range(6))
    kk_ref, ka_ref, rk_ref, lnw_ref, lnb_ref = (next(it) for _ in range(5))
    if has_vres:
        v0_ref, vra_ref, vrb_ref = next(it), next(it), next(it)
    o_ref = next(it)
    vout_ref = None if has_vres else next(it)
    slast_ref = next(it)
    s_scr, carry_scr, y_scr = next(it), next(it), next(it)

    c_idx = pl.program_id(1)
    rd = n_heads * head
    rw = mu_ref.shape[1]
    C = chunk

    @pl.when(c_idx == 0)
    def _():
        s_scr[...] = s0_ref[0]
        carry_scr[...] = shift0_ref[0]

    p = p_ref[:, :rw]
    rid = lax.broadcasted_iota(jnp.int32, p.shape, 0)
    prev = jnp.where(rid == 0, carry_scr[...], pltpu.roll(p, 1, 0))
    carry_scr[...] = p[C - 1:C, :]
    m = p + (prev - p) * mu_ref[...]

    r = m[:, :rd]
    k = m[:, rd:2 * rd]
    v = m[:, 2 * rd:3 * rd]
    pw = m[:, 3 * rd:3 * rd + LANES]
    pa = m[:, 3 * rd + LANES:3 * rd + 2 * LANES]
    pg = m[:, 3 * rd + 2 * LANES:]

    u = w0_ref[...] + _dot(jnp.tanh(pw).astype(BF16), wdec_ref[...])
    lw = (-math.exp(-0.5)) * jax.nn.sigmoid(u)
    a = jax.nn.sigmoid(a0_ref[...] + _dot(pa.astype(BF16), wiclr_ref[...]))
    g = _dot(jax.nn.sigmoid(pg).astype(BF16), wgate_ref[...])
    if has_vres:
        lo = _dot(v.astype(BF16), vra_ref[...])
        v = v + (vfirst_ref[...] - v) * jax.nn.sigmoid(v0_ref[...] + _dot(lo.astype(BF16), vrb_ref[...]))
    else:
        vout_ref[...] = v
    kk_raw = k * kk_ref[...]
    k = k * (1.0 + (a - 1.0) * ka_ref[...])

    row = lax.broadcasted_iota(jnp.int32, (C, C), 0)
    col = lax.broadcasted_iota(jnp.int32, (C, C), 1)
    incl = col <= row
    strict = col < row
    cum = _dot(incl.astype(F32), lw, precision=lax.Precision.HIGHEST)
    dec = jnp.exp(cum)
    dec_prev = jnp.exp(cum - lw)
    dec_inv = jnp.exp(-cum)

    eye = (col == row).astype(F32)
    mask2 = jnp.concatenate([strict, incl], axis=0)

    for h0 in range(0, n_heads, head_group):
        hs = range(h0, min(h0 + head_group, n_heads))
        sls = {h: slice(h * head, (h + 1) * head) for h in hs}
        ar, bk, vb, s_old = {}, {}, {}, {}
        for h in hs:
            sl = sls[h]
            kk = kk_raw[:, sl]
            kk = kk / jnp.maximum(jnp.sqrt(jnp.sum(kk * kk, axis=-1, keepdims=True)), 1e-12)
            a_t = (-kk) * dec_prev[:, sl]
            r_t = r[:, sl] * dec[:, sl]
            b_t = (kk * a[:, sl]) * dec_inv[:, sl]
            k_t = k[:, sl] * dec_inv[:, sl]
            ar[h] = jnp.concatenate([a_t, r_t], axis=0).astype(BF16)
            bk[h] = (b_t.astype(BF16), k_t.astype(BF16))
            vb[h] = v[:, sl].astype(BF16)
            s_old[h] = s_scr[h]
        g_b = {h: jnp.where(mask2, _dot_nt(ar[h], bk[h][0]), 0.0) for h in hs}
        g_k = {h: jnp.where(mask2, _dot_nt(ar[h], bk[h][1]), 0.0).astype(BF16) for h in hs}
        pw_l = {h: g_b[h][:C] for h in hs}
        inv = {h: eye + pw_l[h] for h in hs}
        n = 1
        while 2 * n < C:
            pw_b = {h: pw_l[h].astype(BF16) for h in hs}
            pw_l = {h: _dot(pw_b[h], pw_b[h]) for h in hs}
            inv = {h: inv[h] + _dot(inv[h].astype(BF16), pw_l[h].astype(BF16)) for h in hs}
            n *= 2
        xy = {h: _dot_nt(ar[h], s_old[h].astype(BF16)) + _dot(g_k[h], vb[h]) for h in hs}
        u_b = {h: _dot(inv[h].astype(BF16), xy[h][:C].astype(BF16)).astype(BF16) for h in hs}
        y = {h: xy[h][C:] + _dot(g_b[h][C:].astype(BF16), u_b[h]) for h in hs}
        for h in hs:
            sl = sls[h]
            upd = _dot_tn(jnp.concatenate([u_b[h], vb[h]], axis=0),
                          jnp.concatenate(bk[h], axis=0))
            s_scr[h] = (s_old[h] + upd) * dec[C - 1:C, sl]
            mean = jnp.mean(y[h], axis=-1, keepdims=True)
            yc = y[h] - mean
            var = jnp.mean(yc * yc, axis=-1, keepdims=True)
            yn = yc * lax.rsqrt(var + GN_EPS) * lnw_ref[:, sl] + lnb_ref[:, sl]
            bonus = jnp.sum(r[:, sl] * k[:, sl] * rk_ref[:, sl], axis=-1, keepdims=True) * v[:, sl]
            y_scr[:, sl] = yn + bonus

    o_ref[...] = (y_scr[...] * g).astype(o_ref.dtype)

    @pl.when(c_idx == pl.num_programs(1) - 1)
    def _():
        slast_ref[0] = s_scr[...]


def _rwkv_mixer(p, row0, bsz, t, shift0, s0, vfirst, prm, chunk):
    n_heads, head = s0.shape[1], s0.shape[2]
    rd = n_heads * head
    rw = prm["mu"].shape[1]
    has_vres = vfirst is not None
    nc = t // chunk
    blk0 = row0 // chunk
    np_ = p.shape[1]

    def full(a):
        return pl.BlockSpec(a.shape, lambda b, c: (0,) * a.ndim)

    ins = [p, shift0, s0]
    specs = [pl.BlockSpec((chunk, np_), lambda b, c: (blk0 + b * nc + c, 0)),
             pl.BlockSpec((1, 1, rw), lambda b, c: (b, 0, 0)),
             pl.BlockSpec((1, n_heads, head, head), lambda b, c: (b, 0, 0, 0))]
    if has_vres:
        ins.append(vfirst)
        specs.append(pl.BlockSpec((chunk, rd), lambda b, c: (b * nc + c, 0)))
    names = ["mu", "w0", "wdec", "a0", "wiclr", "wgate", "k_k", "k_a", "r_k", "lnx_w", "lnx_b"]
    if has_vres:
        names += ["v0", "vra", "vrb"]
    for nm in names:
        ins.append(prm[nm])
        specs.append(full(prm[nm]))

    row_spec = pl.BlockSpec((chunk, rd), lambda b, c: (b * nc + c, 0))
    outs = [jax.ShapeDtypeStruct((bsz * t, rd), BF16)]
    out_specs = [row_spec]
    if not has_vres:
        outs.append(jax.ShapeDtypeStruct((bsz * t, rd), F32))
        out_specs.append(row_spec)
    outs.append(jax.ShapeDtypeStruct(s0.shape, F32))
    out_specs.append(pl.BlockSpec((1, n_heads, head, head), lambda b, c: (b, 0, 0, 0)))

    res = pl.pallas_call(
        functools.partial(_rwkv_kernel, n_heads=n_heads, head=head, chunk=chunk, has_vres=has_vres,
                          head_group=n_heads),
        grid=(bsz, nc),
        in_specs=specs,
        out_specs=out_specs,
        out_shape=outs,
        scratch_shapes=[pltpu.VMEM((n_heads, head, head), F32),
                        pltpu.VMEM((1, rw), F32),
                        pltpu.VMEM((chunk, rd), F32)],
        compiler_params=_params("parallel", "arbitrary"),
        name="rwkv_mixer",
    )(*ins)
    if has_vres:
        o, s_last = res
        return o, vfirst, s_last
    o, v_out, s_last = res
    return o, v_out, s_last


def _rotate(x, cos, sin_a, sin_b):
    half = LANES // 4
    return x * cos + pltpu.roll(x, LANES - half, 1) * sin_a + pltpu.roll(x, half, 1) * sin_b


def _mla_proj_kernel(qa_ref, kv_ref, cos_ref, sina_ref, sinb_ref, gqa_ref, wqb_ref, gqn_ref, gqr_ref,
                     gkva_ref, gkr_ref, q_ref, rows_ref, c_ref, kr_ref, *, n_heads, rope, scale):
    cos, sin_a, sin_b = cos_ref[...], sina_ref[...], sinb_ref[...]
    hq = _rms(qa_ref[...], gqa_ref[...]).astype(BF16)
    q = _dot(hq, wqb_ref[...])
    hw = n_heads * LANES
    for h in range(n_heads):
        qn = _rms(q[:, h * LANES:(h + 1) * LANES], gqn_ref[...]) * scale
        qr = _rms(q[:, hw + h * LANES:hw + (h + 1) * LANES], gqr_ref[...], n=rope)
        qr = _rotate(qr, cos, sin_a, sin_b) * scale
        q_ref[:, 2 * h * LANES:(2 * h + 1) * LANES] = qn.astype(q_ref.dtype)
        q_ref[:, (2 * h + 1) * LANES:(2 * h + 2) * LANES] = qr.astype(q_ref.dtype)
    kv = kv_ref[...]
    lat = gkva_ref.shape[1]
    c = _rms(kv[:, :lat], gkva_ref[...])
    kr = _rotate(_rms(kv[:, lat:lat + LANES], gkr_ref[...], n=rope), cos, sin_a, sin_b)
    rows_ref[:, :lat] = c
    rows_ref[:, lat:] = kr[:, :rope]
    c_ref[...] = c.astype(c_ref.dtype)
    kr_ref[...] = kr.astype(kr_ref.dtype)


def _mla_proj(p, col_q, col_kv, tabs, prm, n_heads, rope, scale):
    m = p.shape[0]
    tm = _pick(m, (512, 256, 128, 64, 32, 16, 8))
    qw = prm["g_qa"].shape[1]
    lat = prm["g_kva"].shape[1]
    assert qw == 512 and col_q % qw == 0 and col_kv % qw == 0

    def full(a):
        return pl.BlockSpec(a.shape, lambda i: (0,) * a.ndim)

    tab_spec = pl.BlockSpec((tm, LANES), lambda i: (i, 0))
    ws = [prm["g_qa"], prm["w_qb"], prm["g_qn"], prm["g_qr"], prm["g_kva"], prm["g_kr"]]
    return pl.pallas_call(
        functools.partial(_mla_proj_kernel, n_heads=n_heads, rope=rope, scale=scale),
        grid=(m // tm,),
        in_specs=[pl.BlockSpec((tm, qw), lambda i: (i, col_q // qw)),
                  pl.BlockSpec((tm, qw), lambda i: (i, col_kv // qw)),
                  tab_spec, tab_spec, tab_spec] + [full(a) for a in ws],
        out_specs=[pl.BlockSpec((tm, 2 * n_heads * LANES), lambda i: (i, 0)),
                   pl.BlockSpec((tm, lat + rope), lambda i: (i, 0)),
                   pl.BlockSpec((tm, lat), lambda i: (i, 0)),
                   pl.BlockSpec((tm, LANES), lambda i: (i, 0))],
        out_shape=[jax.ShapeDtypeStruct((m, 2 * n_heads * LANES), BF16),
                   jax.ShapeDtypeStruct((m, lat + rope), F32),
                   jax.ShapeDtypeStruct((m, lat), BF16),
                   jax.ShapeDtypeStruct((m, LANES), BF16)],
        compiler_params=_params("parallel"),
        name="mla_proj",
    )(p, p, *tabs, *ws)


def _kv_expand_kernel(c_ref, kr_ref, wuk_ref, wuv_ref, gkn_ref, k_ref, v_ref, *, n_heads):
    c = c_ref[...]
    ke = _dot(c, wuk_ref[...])
    kr = kr_ref[...]
    for h in range(n_heads):
        kh = _rms(ke[:, h * LANES:(h + 1) * LANES], gkn_ref[...])
        k_ref[:, 2 * h * LANES:(2 * h + 1) * LANES] = kh.astype(k_ref.dtype)
        k_ref[:, (2 * h + 1) * LANES:(2 * h + 2) * LANES] = kr
    v_ref[...] = _dot_nt(wuv_ref[...], c).astype(v_ref.dtype)


def _kv_expand(c, kr, mp, wuk, wuv_t, gkn, n_heads):
    lat = c.shape[1]
    tm = _pick(mp, (512, 256, 128))
    assert wuk.shape[1] == n_heads * LANES and wuv_t.shape[0] == n_heads * LANES

    def full(a):
        return pl.BlockSpec(a.shape, lambda i: (0,) * a.ndim)

    return pl.pallas_call(
        functools.partial(_kv_expand_kernel, n_heads=n_heads),
        grid=(mp // tm,),
        in_specs=[pl.BlockSpec((tm, lat), lambda i: (i, 0)),
                  pl.BlockSpec((tm, LANES), lambda i: (i, 0)),
                  full(wuk), full(wuv_t), full(gkn)],
        out_specs=[pl.BlockSpec((tm, 2 * n_heads * LANES), lambda i: (i, 0)),
                   pl.BlockSpec((n_heads * LANES, tm), lambda i: (0, i))],
        out_shape=[jax.ShapeDtypeStruct((mp, 2 * n_heads * LANES), BF16),
                   jax.ShapeDtypeStruct((n_heads * LANES, mp), BF16)],
        compiler_params=_params("parallel"),
        name="kv_expand",
    )(c, kr, wuk, wuv_t, gkn)


def _flash_kernel(qi_ref, ki_ref, q_ref, k_ref, v_ref, o_ref, m_scr, l_scr, acc_scr, *, tq, tk):
    pair = pl.program_id(2)
    qi, ki = qi_ref[pair], ki_ref[pair]

    @pl.when(ki == 0)
    def _():
        m_scr[...] = jnp.full(m_scr.shape, -jnp.inf, F32)
        l_scr[...] = jnp.zeros(l_scr.shape, F32)
        acc_scr[...] = jnp.zeros(acc_scr.shape, F32)

    s = _dot_nt(k_ref[...], q_ref[...])
    key = ki * tk + lax.broadcasted_iota(jnp.int32, s.shape, 0)
    qry = qi * tq + lax.broadcasted_iota(jnp.int32, s.shape, 1)
    s = jnp.where(key <= qry, s, -jnp.inf)
    m_old = m_scr[...]
    m_new = jnp.maximum(m_old, jnp.max(s, axis=0, keepdims=True))
    alpha = jnp.exp(m_old - m_new)
    p = jnp.exp(s - m_new)
    l_scr[...] = alpha * l_scr[...] + jnp.sum(p, axis=0, keepdims=True)
    acc_scr[...] = alpha * acc_scr[...] + _dot(v_ref[...], p.astype(BF16))
    m_scr[...] = m_new

    @pl.when((ki + 1) * tk >= (qi + 1) * tq)
    def _():
        o_ref[...] = (acc_scr[...] / l_scr[...]).T.astype(o_ref.dtype)


def _flash_prompt(q, k, v, bsz, t, n_heads):
    tk = _pick(t, (512, 256, 128))
    tq = 2 * tk if t % (2 * tk) == 0 else tk
    nq, nk = t // tq, t // tk
    pairs = [(qi, ki) for qi in range(nq) for ki in range((qi + 1) * tq // tk)]
    qi_arr = jnp.asarray([pr[0] for pr in pairs], jnp.int32)
    ki_arr = jnp.asarray([pr[1] for pr in pairs], jnp.int32)
    grid_spec = pltpu.PrefetchScalarGridSpec(
        num_scalar_prefetch=2,
        grid=(bsz, n_heads, len(pairs)),
        in_specs=[pl.BlockSpec((tq, 2 * LANES), lambda b, h, pr, qa, ka: (b * nq + qa[pr], h)),
                  pl.BlockSpec((tk, 2 * LANES), lambda b, h, pr, qa, ka: (b * nk + ka[pr], h)),
                  pl.BlockSpec((LANES, tk), lambda b, h, pr, qa, ka: (h, b * nk + ka[pr]))],
        out_specs=pl.BlockSpec((tq, LANES), lambda b, h, pr, qa, ka: (b * nq + qa[pr], h)),
        scratch_shapes=[pltpu.VMEM((1, tq), F32), pltpu.VMEM((1, tq), F32), pltpu.VMEM((LANES, tq), F32)],
    )
    return pl.pallas_call(
        functools.partial(_flash_kernel, tq=tq, tk=tk),
        grid_spec=grid_spec,
        out_shape=jax.ShapeDtypeStruct((bsz * t, n_heads * LANES), BF16),
        compiler_params=_params("parallel", "parallel", "arbitrary"),
        name="flash_prompt",
    )(qi_arr, ki_arr, q, k, v)


def _sample_attn_kernel(pt_ref, q_ref, cn_ref, krn_ref, wuk3_ref, wukt_ref, wuv3_ref, gkn_ref, *rest,
                        n_heads, n_tok, n_page_in, sub_pages, lat, rope):
    page_refs = rest[:n_page_in]
    o_ref = rest[n_page_in]
    lhs_scr, qr_scr, m_scr, l_scr, acc_scr, s_scr, c_scr = rest[n_page_in + 1:]
    j = pl.program_id(1)
    nq = n_heads * n_tok
    hk = n_heads * LANES

    @pl.when(j == 0)
    def _():
        lhs_scr[:hk, :] = wukt_ref[...]
        for h in range(n_heads):
            qn = (q_ref[:, 2 * h * LANES:(2 * h + 1) * LANES].astype(F32) * gkn_ref[...]).astype(BF16)
            lhs_scr[hk + h * n_tok:hk + (h + 1) * n_tok, :] = _dot_nt(qn, wuk3_ref[h]).astype(BF16)
            qr_scr[h * n_tok:(h + 1) * n_tok, :] = q_ref[:, (2 * h + 1) * LANES:(2 * h + 2) * LANES]
        m_scr[...] = jnp.full(m_scr.shape, -jnp.inf, F32)
        l_scr[...] = jnp.zeros(l_scr.shape, F32)
        acc_scr[...] = jnp.zeros(acc_scr.shape, F32)
        s_scr[...] = jnp.full(s_scr.shape, NEUTRAL_SCORE, F32)
        c_scr[...] = jnp.zeros(c_scr.shape, BF16)

    def scores(big, s_rope):
        npos = big.shape[1]
        invs = []
        for h in range(n_heads):
            kh = big[h * LANES:(h + 1) * LANES]
            ms = jnp.sum(kh * kh, axis=0, keepdims=True) * (1.0 / LANES)
            invs.append(jnp.broadcast_to(lax.rsqrt(ms + RMS_EPS), (n_tok, npos)))
        return big[hk:] * jnp.concatenate(invs, axis=0) + s_rope

    diag = (lax.broadcasted_iota(jnp.int32, (nq, nq), 0) == lax.broadcasted_iota(jnp.int32, (nq, nq), 1))

    def as_row(col):
        return jnp.sum(jnp.where(diag, col, 0.0), axis=0, keepdims=True)

    def load_tile(sub):
        pages = [page_refs[sub * sub_pages + i][0, 0] for i in range(sub_pages)]
        c_t = jnp.concatenate([pg[:lat].astype(BF16) for pg in pages], axis=1)
        kr_t = jnp.concatenate([pg[lat:lat + rope].astype(BF16) for pg in pages], axis=1)
        return c_t, _dot(lhs_scr[...], c_t), _dot(qr_scr[:, :rope], kr_t)

    def softmax_update():
        s = s_scr[...]
        m_old = m_scr[...]
        m_new = jnp.maximum(m_old, jnp.max(s, axis=-1, keepdims=True))
        alpha = jnp.exp(m_old - m_new)
        p = jnp.exp(s - m_new)
        l_scr[...] = alpha * l_scr[...] + jnp.sum(p, axis=-1, keepdims=True)
        acc_scr[...] = as_row(alpha) * acc_scr[...] + _dot_nt(c_scr[...], p.astype(BF16))
        m_scr[...] = m_new

    n_sub = n_page_in // sub_pages
    nxt = load_tile(0)
    softmax_update()
    for sub in range(n_sub):
        c_t, big, s_rope = nxt
        if sub + 1 < n_sub:
            nxt = load_tile(sub + 1)
        npos = c_t.shape[1]
        s_scr[:, sub * npos:(sub + 1) * npos] = scores(big, s_rope)
        c_scr[:, sub * npos:(sub + 1) * npos] = c_t

    @pl.when(j == pl.num_programs(1) - 1)
    def _():
        softmax_update()
        c_n = cn_ref[...]
        row = lax.broadcasted_iota(jnp.int32, (nq, n_tok), 0)
        col = lax.broadcasted_iota(jnp.int32, (nq, n_tok), 1)
        s = scores(_dot_nt(lhs_scr[...], c_n), _dot_nt(qr_scr[:, :rope], krn_ref[:, :rope]))
        s = jnp.where(col <= (row % n_tok), s, -jnp.inf)
        m_old = m_scr[...]
        m_new = jnp.maximum(m_old, jnp.max(s, axis=-1, keepdims=True))
        alpha = jnp.exp(m_old - m_new)
        p = jnp.exp(s - m_new)
        l_fin = alpha * l_scr[...] + jnp.sum(p, axis=-1, keepdims=True)
        acc_fin = (as_row(alpha) * acc_scr[...]).T + _dot(p.astype(BF16), c_n)
        o_lat = (acc_fin / l_fin).astype(BF16)
        for h in range(n_heads):
            o_h = _dot(o_lat[h * n_tok:(h + 1) * n_tok], wuv3_ref[h])
            o_ref[:, h * LANES:(h + 1) * LANES] = o_h.astype(o_ref.dtype)


def _sample_attn(q, c_new, kr_new, row0, cache_t, layer, page_table, wuk3, wukt, wuv3, gkn,
                 n_tok, n_heads, rope):
    bs, n_pages = page_table.shape
    lat = c_new.shape[1]
    kvw, page = cache_t.shape[2], cache_t.shape[3]
    n_page_in = _pick(n_pages, (16, 8, 4, 2, 1))
    sub_pages = _pick(n_page_in, (4, 2, 1))
    steps = n_pages // n_page_in
    blk0 = row0 // n_tok
    nq = n_heads * n_tok

    def full(a):
        return pl.BlockSpec(a.shape, lambda b, j, pt: (0,) * a.ndim)

    def page_spec(i):
        return pl.BlockSpec((1, 1, kvw, page),
                            lambda b, j, pt: (pt[b * n_pages + j * n_page_in + i], layer, 0, 0))

    grid_spec = pltpu.PrefetchScalarGridSpec(
        num_scalar_prefetch=1,
        grid=(bs, steps),
        in_specs=[pl.BlockSpec((n_tok, 2 * n_heads * LANES), lambda b, j, pt: (blk0 + b, 0)),
                  pl.BlockSpec((n_tok, lat), lambda b, j, pt: (blk0 + b, 0)),
                  pl.BlockSpec((n_tok, LANES), lambda b, j, pt: (blk0 + b, 0)),
                  full(wuk3), full(wukt), full(wuv3), full(gkn)] + [page_spec(i) for i in range(n_page_in)],
        out_specs=pl.BlockSpec((n_tok, n_heads * LANES), lambda b, j, pt: (b, 0)),
        scratch_shapes=[pltpu.VMEM((n_heads * LANES + nq, lat), BF16), pltpu.VMEM((nq, LANES), BF16),
                        pltpu.VMEM((nq, 1), F32), pltpu.VMEM((nq, 1), F32), pltpu.VMEM((lat, nq), F32),
                        pltpu.VMEM((nq, n_page_in * page), F32), pltpu.VMEM((lat, n_page_in * page), BF16)],
    )
    return pl.pallas_call(
        functools.partial(_sample_attn_kernel, n_heads=n_heads, n_tok=n_tok, n_page_in=n_page_in,
                          sub_pages=sub_pages, lat=lat, rope=rope),
        grid_spec=grid_spec,
        out_shape=jax.ShapeDtypeStruct((bs * n_tok, n_heads * LANES), BF16),
        compiler_params=_params("parallel", "arbitrary"),
        name="sample_attn",
    )(page_table.reshape(-1), q, c_new, kr_new, wuk3, wukt, wuv3, gkn, *([cache_t] * n_page_in))


def _out_proj_kernel(x_ref, a_ref, b_ref, w_ref, o_ref):
    ka = a_ref.shape[1]
    o_ref[...] = x_ref[...] + _dot(a_ref[...], w_ref[:ka, :]) + _dot(b_ref[...], w_ref[ka:, :])


def _out_proj(x, a, b, w):
    m, d = x.shape
    tm = _pick(m, (1024, 512, 256, 128, 64, 32, 16, 8))
    tn = _pick(d, (512, 256, 128))
    return pl.pallas_call(
        _out_proj_kernel,
        grid=(m // tm, d // tn),
        in_specs=[pl.BlockSpec((tm, tn), lambda i, j: (i, j)),
                  pl.BlockSpec((tm, a.shape[1]), lambda i, j: (i, 0)),
                  pl.BlockSpec((tm, b.shape[1]), lambda i, j: (i, 0)),
                  pl.BlockSpec((w.shape[0], tn), lambda i, j: (0, j))],
        out_specs=pl.BlockSpec((tm, tn), lambda i, j: (i, j)),
        out_shape=jax.ShapeDtypeStruct((m, d), F32),
        compiler_params=_params("parallel", "parallel"),
        name="out_proj",
    )(x, a, b, w)


def _ffn_kernel(x_ref, g_ref, wg_ref, wu_ref, wd_ref, o_ref, hn_ref):
    @pl.when(pl.program_id(1) == 0)
    def _():
        x = x_ref[...]
        hn_ref[...] = _rms(x, g_ref[...]).astype(BF16)
        o_ref[...] = x

    h = hn_ref[...]
    gt = _dot(h, wg_ref[...])
    act = (gt * jax.nn.sigmoid(gt) * _dot(h, wu_ref[...])).astype(BF16)
    o_ref[...] += _dot(act, wd_ref[...])


def _ffn_dense(x, g, wg, wu, wd):
    m, d = x.shape
    f = wg.shape[1]
    tm = _pick(m, (512, 256, 128, 64, 32, 16, 8))
    tf = _pick(f, (512, 256, 128))
    return pl.pallas_call(
        _ffn_kernel,
        grid=(m // tm, f // tf),
        in_specs=[pl.BlockSpec((tm, d), lambda i, j: (i, 0)),
                  pl.BlockSpec((1, d), lambda i, j: (0, 0)),
                  pl.BlockSpec((d, tf), lambda i, j: (0, j)),
                  pl.BlockSpec((d, tf), lambda i, j: (0, j)),
                  pl.BlockSpec((tf, d), lambda i, j: (j, 0))],
        out_specs=pl.BlockSpec((tm, d), lambda i, j: (i, 0)),
        out_shape=jax.ShapeDtypeStruct((m, d), F32),
        scratch_shapes=[pltpu.VMEM((tm, d), BF16)],
        compiler_params=_params("parallel", "arbitrary"),
        name="ffn_dense",
    )(x, g, wg, wu, wd)


def _router_kernel(x_ref, g_ref, wr_ref, hn_ref, lg_ref):
    hn = _rms(x_ref[...], g_ref[...])
    hn_ref[...] = hn.astype(hn_ref.dtype)
    lg_ref[...] = _dot(hn, wr_ref[...], precision=lax.Precision.HIGHEST)


def _router(x, g, wr):
    m, d = x.shape
    tm = _pick(m, (512, 256, 128, 64, 32, 16, 8))
    return pl.pallas_call(
        _router_kernel,
        grid=(m // tm,),
        in_specs=[pl.BlockSpec((tm, d), lambda i: (i, 0)),
                  pl.BlockSpec((1, d), lambda i: (0, 0)),
                  pl.BlockSpec(wr.shape, lambda i: (0, 0))],
        out_specs=[pl.BlockSpec((tm, d), lambda i: (i, 0)),
                   pl.BlockSpec((tm, wr.shape[1]), lambda i: (i, 0))],
        out_shape=[jax.ShapeDtypeStruct((m, d), BF16),
                   jax.ShapeDtypeStruct((m, wr.shape[1]), F32)],
        compiler_params=_params("parallel"),
        name="router",
    )(x, g, wr)


def _experts_kernel(blk_e_ref, n_used_ref, xs_ref, gate_ref, wg_ref, wu_ref, wd_ref, o_ref):
    i, j = pl.program_id(0), pl.program_id(1)

    @pl.when(j == 0)
    def _():
        o_ref[...] = jnp.zeros(o_ref.shape, o_ref.dtype)

    @pl.when(i < n_used_ref[0])
    def _():
        x = xs_ref[...]
        gt = _dot(x, wg_ref[0].astype(BF16))
        act = (gt * jax.nn.sigmoid(gt) * _dot(x, wu_ref[0].astype(BF16))).astype(BF16)
        o_ref[...] += _dot(act, wd_ref[0].astype(BF16))

    @pl.when(j == pl.num_programs(1) - 1)
    def _():
        o_ref[...] = o_ref[...] * gate_ref[...]


def _experts(xs, row_gate, blk_e, n_used, wg, wu, wd, bm):
    n_rows, d = xs.shape
    f = wg.shape[2]
    tf = _pick(f, (256, 128))
    grid_spec = pltpu.PrefetchScalarGridSpec(
        num_scalar_prefetch=2,
        grid=(n_rows // bm, f // tf),
        in_specs=[pl.BlockSpec((bm, d), lambda i, j, be, nu: (i, 0)),
                  pl.BlockSpec((bm, 1), lambda i, j, be, nu: (i, 0)),
                  pl.BlockSpec((1, d, tf), lambda i, j, be, nu: (be[i], 0, j)),
                  pl.BlockSpec((1, d, tf), lambda i, j, be, nu: (be[i], 0, j)),
                  pl.BlockSpec((1, tf, d), lambda i, j, be, nu: (be[i], j, 0))],
        out_specs=pl.BlockSpec((bm, d), lambda i, j, be, nu: (i, 0)),
    )
    return pl.pallas_call(
        _experts_kernel,
        grid_spec=grid_spec,
        out_shape=jax.ShapeDtypeStruct((n_rows, d), F32),
        compiler_params=_params("parallel", "arbitrary"),
        name="experts",
    )(blk_e, n_used, xs, row_gate, wg, wu, wd)


def _moe(x, g, wr, wg, wu, wd, splits):
    m, d = x.shape
    n_exp = wg.shape[0]
    wr_pad = jnp.pad(wr, ((0, 0), (0, LANES - n_exp)))
    hn, logits = _router(x, g, wr_pad)
    top_logit, top_e = lax.top_k(logits[:, :n_exp], TOP_K)
    gates = jax.nn.softmax(top_logit, axis=-1)
    n_assign = m * TOP_K
    bm = _pick(m, (1024, 512, 256, 128, 64, 32, 16, 8))
    flat_e = top_e.reshape(-1)
    onehot = (flat_e[:, None] == jnp.arange(n_exp, dtype=flat_e.dtype)[None, :]).astype(jnp.int32)
    csum = jnp.cumsum(onehot, axis=0)
    counts = csum[-1]
    rank = jnp.sum((csum - onehot) * onehot, axis=1)
    padded = (counts + bm - 1) // bm * bm
    pad_end = jnp.cumsum(padded)
    pad_start = pad_end - padded
    slot = pad_start[flat_e] + rank
    n_rows = _round_up(n_assign + n_exp * (bm - 1), bm)
    n_blocks = n_rows // bm
    flat_tok = jnp.repeat(jnp.arange(m, dtype=jnp.int32), TOP_K)
    assert m < 2 ** 24
    fill = jnp.broadcast_to(jnp.asarray([m, 0.0], F32), (n_rows, 2))
    row_info = fill.at[slot].set(jnp.stack([flat_tok.astype(F32), gates.reshape(-1)], axis=1))
    row_tok = row_info[:, 0].astype(jnp.int32)
    row_gate = row_info[:, 1]
    blk_start = jnp.arange(n_blocks, dtype=jnp.int32) * bm
    blk_e = jnp.minimum(jnp.sum(pad_end[None, :] <= blk_start[:, None], axis=1), n_exp - 1).astype(jnp.int32)
    n_used = (pad_end[-1] // bm).astype(jnp.int32).reshape(1)
    hn_pad = jnp.concatenate([hn, jnp.zeros((1, d), hn.dtype)], axis=0)
    xs = hn_pad[row_tok]
    yb = _experts(xs, row_gate[:, None], blk_e, n_used, wg, wu, wd, bm)
    slot2 = slot.reshape(m, TOP_K)
    return [x[a:b] + yb[slot2[a:b, 0]] + yb[slot2[a:b, 1]] for a, b in splits]


def _pad_cols(a, width):
    return jnp.pad(a, [(0, 0)] * (a.ndim - 1) + [(0, width - a.shape[-1])])


def _rwkv_cols(a, rd, wl, al):
    o = 3 * rd
    return jnp.concatenate([a[..., :o], _pad_cols(a[..., o:o + wl], LANES),
                            _pad_cols(a[..., o + wl:o + wl + al], LANES),
                            _pad_cols(a[..., o + wl + al:], 2 * LANES)], axis=-1)


def _rwkv_cols_inv(a, rd, wl, al, gl):
    o = 3 * rd
    return jnp.concatenate([a[..., :o], a[..., o:o + wl], a[..., o + LANES:o + LANES + al],
                            a[..., o + 2 * LANES:o + 2 * LANES + gl]], axis=-1)


def _pad_rows(a, rows):
    return jnp.pad(a, ((0, rows - a.shape[0]), (0, 0)))


def _rope_tables(pos, rope):
    half = rope // 2
    assert half == LANES // 4
    inv_freq = ROPE_THETA ** (-jnp.arange(half, dtype=F32) * (2.0 / rope))
    ang = pos.astype(F32)[:, None] * inv_freq[None, :]
    cos, sin = jnp.cos(ang), jnp.sin(ang)
    z = jnp.zeros_like(cos)
    return (jnp.concatenate([cos, cos, z, z], axis=1),
            jnp.concatenate([-sin, z, z, z], axis=1),
            jnp.concatenate([z, sin, z, z], axis=1))


def kernel(x_prompt, x_sample, cache_kv, page_table, state_wkv, state_shift, g_mix, w_in, mu_shift, w0, w_decay, a0, w_iclr, w_gate_out, k_k, k_a, r_k, lnx_w, lnx_b, v0, v_res_a, v_res_b, g_qa, w_qb, g_qn, g_qr, g_kva, g_kr, w_uk, w_uv, g_kn, w_out, g_ffn, w_ffn_gate, w_ffn_up, w_ffn_down, w_router, w_exp_gate, w_exp_up, w_exp_down):
    bp, tp, d = x_prompt.shape
    bs, ts, _ = x_sample.shape
    depth = g_mix.shape[0]
    n_heads_r, head_r = r_k.shape[1], r_k.shape[2]
    rd = n_heads_r * head_r
    wl, al, gl = w_decay.shape[1], w_iclr.shape[1], w_gate_out.shape[1]
    rw_orig = 3 * rd + wl + al + gl
    ql, lat, rope, nope = g_qa.shape[1], g_kva.shape[1], g_kr.shape[1], g_qn.shape[1]
    n_heads_a, mla_v = w_uk.shape[2], w_uv.shape[3]
    assert nope == LANES and mla_v == LANES and rope == LANES // 2
    assert wl <= LANES and al <= LANES and gl <= 2 * LANES and v_res_a.shape[2] <= LANES
    past_len = page_table.shape[1] * PAGE_SIZE
    mp, ms = bp * tp, bs * ts
    scale = (nope + rope) ** -0.5
    rw = 3 * rd + 4 * LANES
    col_q, col_kv = rw, rw + ql
    chunk_p = _pick(tp, (64, 32, 16, 8))
    chunk_s = _pick(ts, (64, 32, 16, 8))

    x = jnp.concatenate([x_prompt.reshape(mp, d), x_sample.reshape(ms, d)], axis=0)
    pos = jnp.concatenate([jnp.tile(jnp.arange(tp, dtype=jnp.int32), bp),
                           jnp.tile(past_len + jnp.arange(ts, dtype=jnp.int32), bs)])
    tabs = _rope_tables(pos, rope)
    row2 = lambda a: a.reshape(1, -1)
    cache_t = jnp.transpose(cache_kv, (0, 1, 3, 2))

    shift0_p = jnp.zeros((bp, 1, rw), F32)
    wkv0_p = jnp.zeros((bp, n_heads_r, head_r, head_r), F32)
    vf_p = vf_s = None
    rows_p, rows_s, wkv_p, wkv_s, sh_p, sh_s = [], [], [], [], [], []
    for l in range(depth):
        w_in_l = w_in[l]
        w_pad = jnp.concatenate([
            _rwkv_cols(w_in_l[:, :rw_orig], rd, wl, al),
            w_in_l[:, rw_orig:rw_orig + ql + lat],
            _pad_cols(w_in_l[:, rw_orig + ql + lat:], 2 * LANES)], axis=1).astype(BF16)
        p = _in_proj(x, row2(g_mix[l]), w_pad)

        prm = {
            "mu": _rwkv_cols(row2(mu_shift[l]), rd, wl, al),
            "w0": row2(w0[l]), "a0": row2(a0[l]),
            "wdec": _pad_rows(w_decay[l], LANES).astype(BF16),
            "wiclr": _pad_rows(w_iclr[l], LANES).astype(BF16),
            "wgate": _pad_rows(w_gate_out[l], 2 * LANES).astype(BF16),
            "k_k": row2(k_k[l]), "k_a": row2(k_a[l]), "r_k": row2(r_k[l]),
            "lnx_w": row2(lnx_w[l]), "lnx_b": row2(lnx_b[l]),
        }
        if l > 0:
            prm["v0"] = row2(v0[l - 1])
            prm["vra"] = _pad_cols(v_res_a[l - 1], LANES).astype(BF16)
            prm["vrb"] = _pad_rows(v_res_b[l - 1], LANES).astype(BF16)
        oa_p, vf_p, wp = _rwkv_mixer(p, 0, bp, tp, shift0_p, wkv0_p, vf_p, prm, chunk_p)
        shift0_s = _rwkv_cols(state_shift[l], rd, wl, al)[:, None, :]
        oa_s, vf_s, ws = _rwkv_mixer(p, mp, bs, ts, shift0_s, state_wkv[l], vf_s, prm, chunk_s)

        wqb3 = w_qb[l].reshape(ql, n_heads_a, nope + rope)
        wqb_pad = jnp.concatenate([wqb3[:, :, :nope].reshape(ql, n_heads_a * nope),
                                   _pad_cols(wqb3[:, :, nope:], LANES).reshape(ql, n_heads_a * LANES)],
                                  axis=1).astype(BF16)
        mprm = {"g_qa": row2(g_qa[l]), "w_qb": wqb_pad, "g_qn": row2(g_qn[l]),
                "g_qr": _pad_cols(row2(g_qr[l]), LANES), "g_kva": row2(g_kva[l]),
                "g_kr": _pad_cols(row2(g_kr[l]), LANES)}
        q_cat, rows, c_bf, kr_bf = _mla_proj(p, col_q, col_kv, tabs, mprm, n_heads_a, rope, scale)

        wuk2 = w_uk[l].reshape(lat, n_heads_a * nope).astype(BF16)
        wuv_t = jnp.transpose(w_uv[l], (1, 2, 0)).reshape(n_heads_a * mla_v, lat).astype(BF16)
        gkn = row2(g_kn[l])
        k_cat, v_exp = _kv_expand(c_bf, kr_bf, mp, wuk2, wuv_t, gkn, n_heads_a)
        ob_p = _flash_prompt(q_cat, k_cat, v_exp, bp, tp, n_heads_a)

        wuk3 = jnp.transpose(w_uk[l], (1, 0, 2)).astype(BF16)
        wukt = jnp.transpose(w_uk[l], (1, 2, 0)).reshape(n_heads_a * nope, lat).astype(BF16)
        wuv3 = jnp.transpose(w_uv[l], (1, 0, 2)).astype(BF16)
        ob_s = _sample_attn(q_cat, c_bf, kr_bf, mp, cache_t, l, page_table, wuk3, wukt, wuv3, gkn,
                            ts, n_heads_a, rope)

        o_a = jnp.concatenate([oa_p, oa_s], axis=0)
        o_b = jnp.concatenate([ob_p, ob_s], axis=0)
        x = _out_proj(x, o_a, o_b, w_out[l].astype(BF16))

        i = l // 2
        if l % 2 == 0:
            x = _ffn_dense(x, row2(g_ffn[l]), w_ffn_gate[i].astype(BF16), w_ffn_up[i].astype(BF16),
                           w_ffn_down[i].astype(BF16))
        else:
            last = l == depth - 1
            parts = _moe(x, row2(g_ffn[l]), w_router[i], w_exp_gate[i], w_exp_up[i], w_exp_down[i],
                         [(0, mp), (mp, mp + ms)] if last else [(0, mp + ms)])
            x = None if last else parts[0]

        rows_p.append(rows[:mp].reshape(bp, tp, lat + rope))
        rows_s.append(rows[mp:].reshape(bs, ts, lat + rope))
        wkv_p.append(wp)
        wkv_s.append(ws)
        last_p = p[tp - 1:mp:tp, :rw]
        last_s = p[mp + ts - 1::ts, :rw]
        sh_p.append(_rwkv_cols_inv(last_p, rd, wl, al, gl))
        sh_s.append(_rwkv_cols_inv(last_s, rd, wl, al, gl))

    if x is not None:
        parts = [x[:mp], x[mp:]]
    return (parts[0].reshape(bp, tp, d), parts[1].reshape(bs, ts, d),
            jnp.stack(rows_p), jnp.stack(rows_s), jnp.stack(wkv_p), jnp.stack(wkv_s),
            jnp.stack(sh_p), jnp.stack(sh_s))
```

```python
import functools
import math

import jax
import jax.numpy as jnp
from jax import lax
from jax.experimental import pallas as pl
from jax.experimental.pallas import tpu as pltpu

F32 = jnp.float32
BF16 = jnp.bfloat16
RMS_EPS = 1e-6
GN_EPS = 64e-5
ROPE_THETA = 10000.0
TOP_K = 2
PAGE_SIZE = 128
LANES = 128
VMEM_LIMIT = 56 * 1024 * 1024
NEUTRAL_SCORE = -1e30


def _pick(n, prefs):
    for p in prefs:
        if n % p == 0:
            return p
    raise ValueError(f"no tile for {n} in {prefs}")


def _round_up(n, m):
    return (n + m - 1) // m * m


def _params(*sem):
    return pltpu.CompilerParams(dimension_semantics=sem, vmem_limit_bytes=VMEM_LIMIT)


def _dot(a, b, **kw):
    return jnp.dot(a, b, preferred_element_type=F32, **kw)


def _dot_nt(a, b):
    return lax.dot_general(a, b, (((1,), (1,)), ((), ())), preferred_element_type=F32)


def _dot_tn(a, b):
    return lax.dot_general(a, b, (((0,), (0,)), ((), ())), preferred_element_type=F32)


def _rms(x, g, n=None):
    n = x.shape[-1] if n is None else n
    ms = jnp.sum(x * x, axis=-1, keepdims=True) * (1.0 / n)
    return x * lax.rsqrt(ms + RMS_EPS) * g


def _in_proj_kernel(x_ref, g_ref, w_ref, o_ref, hn_ref):
    @pl.when(pl.program_id(1) == 0)
    def _():
        hn_ref[...] = _rms(x_ref[...], g_ref[...]).astype(BF16)

    o_ref[...] = _dot(hn_ref[...], w_ref[...])


def _in_proj(x, g, w):
    m, d = x.shape
    n = w.shape[1]
    tm = _pick(m, (1024, 512, 256, 128, 64, 32, 16, 8))
    tn = _pick(n, (1536, 512, 256, 128))
    return pl.pallas_call(
        _in_proj_kernel,
        grid=(m // tm, n // tn),
        in_specs=[pl.BlockSpec((tm, d), lambda i, j: (i, 0)),
                  pl.BlockSpec((1, d), lambda i, j: (0, 0)),
                  pl.BlockSpec((d, tn), lambda i, j: (0, j))],
        out_specs=pl.BlockSpec((tm, tn), lambda i, j: (i, j)),
        out_shape=jax.ShapeDtypeStruct((m, n), F32),
        scratch_shapes=[pltpu.VMEM((tm, d), BF16)],
        compiler_params=_params("parallel", "arbitrary"),
        name="in_proj",
    )(x, g, w)


def _rwkv_kernel(*refs, n_heads, head, chunk, has_vres, head_group):
    it = iter(refs)
    p_ref, shift0_ref, s0_ref = next(it), next(it), next(it)
    vfirst_ref = next(it) if has_vres else None
    mu_ref, w0_ref, wdec_ref, a0_ref, wiclr_ref, wgate_ref = (next(it) for _ in range(6))
    kk_ref, ka_ref, rk_ref, lnw_ref, lnb_ref = (next(it) for _ in range(5))
    if has_vres:
        v0_ref, vra_ref, vrb_ref = next(it), next(it), next(it)
    o_ref = next(it)
    vout_ref = None if has_vres else next(it)
    slast_ref = next(it)
    s_scr, carry_scr, y_scr = next(it), next(it), next(it)

    c_idx = pl.program_id(1)
    rd = n_heads * head
    rw = mu_ref.shape[1]
    C = chunk

    @pl.when(c_idx == 0)
    def _():
        s_scr[...] = s0_ref[0]
        carry_scr[...] = shift0_ref[0]

    p = p_ref[:, :rw]
    rid = lax.broadcasted_iota(jnp.int32, p.shape, 0)
    prev = jnp.where(rid == 0, carry_scr[...], pltpu.roll(p, 1, 0))
    carry_scr[...] = p[C - 1:C, :]
    m = p + (prev - p) * mu_ref[...]

    r = m[:, :rd]
    k = m[:, rd:2 * rd]
    v = m[:, 2 * rd:3 * rd]
    pw = m[:, 3 * rd:3 * rd + LANES]
    pa = m[:, 3 * rd + LANES:3 * rd + 2 * LANES]
    pg = m[:, 3 * rd + 2 * LANES:]

    u = w0_ref[...] + _dot(jnp.tanh(pw).astype(BF16), wdec_ref[...])
    lw = (-math.exp(-0.5)) * jax.nn.sigmoid(u)
    a = jax.nn.sigmoid(a0_ref[...] + _dot(pa.astype(BF16), wiclr_ref[...]))
    g = _dot(jax.nn.sigmoid(pg).astype(BF16), wgate_ref[...])
    if has_vres:
        lo = _dot(v.astype(BF16), vra_ref[...])
        v = v + (vfirst_ref[...] - v) * jax.nn.sigmoid(v0_ref[...] + _dot(lo.astype(BF16), vrb_ref[...]))
    else:
        vout_ref[...] = v
    kk_raw = k * kk_ref[...]
    k = k * (1.0 + (a - 1.0) * ka_ref[...])

    row = lax.broadcasted_iota(jnp.int32, (C, C), 0)
    col = lax.broadcasted_iota(jnp.int32, (C, C), 1)
    incl = col <= row
    strict = col < row
    cum = _dot(incl.astype(F32), lw, precision=lax.Precision.HIGHEST)
    dec = jnp.exp(cum)
    dec_prev = jnp.exp(cum - lw)
    dec_inv = jnp.exp(-cum)

    eye = (col == row).astype(F32)
    mask2 = jnp.concatenate([strict, incl], axis=0)

    for h0 in range(0, n_heads, head_group):
        hs = range(h0, min(h0 + head_group, n_heads))
        sls = {h: slice(h * head, (h + 1) * head) for h in hs}
        ar, bk, vb, s_old = {}, {}, {}, {}
        for h in hs:
            sl = sls[h]
            kk = kk_raw[:, sl]
            kk = kk / jnp.maximum(jnp.sqrt(jnp.sum(kk * kk, axis=-1, keepdims=True)), 1e-12)
            a_t = (-kk) * dec_prev[:, sl]
            r_t = r[:, sl] * dec[:, sl]
            b_t = (kk * a[:, sl]) * dec_inv[:, sl]
            k_t = k[:, sl] * dec_inv[:, sl]
            ar[h] = jnp.concatenate([a_t, r_t], axis=0).astype(BF16)
            bk[h] = (b_t.astype(BF16), k_t.astype(BF16))
            vb[h] = v[:, sl].astype(BF16)
            s_old[h] = s_scr[h]
        g_b = {h: jnp.where(mask2, _dot_nt(ar[h], bk[h][0]), 0.0) for h in hs}
        g_k = {h: jnp.where(mask2, _dot_nt(ar[h], bk[h][1]), 0.0).astype(BF16) for h in hs}
        pw_l = {h: g_b[h][:C] for h in hs}
        inv = {h: eye + pw_l[h] for h in hs}
        n = 1
        while 2 * n < C:
            pw_b = {h: pw_l[h].astype(BF16) for h in hs}
            pw_l = {h: _dot(pw_b[h], pw_b[h]) for h in hs}
            inv = {h: inv[h] + _dot(inv[h].astype(BF16), pw_l[h].astype(BF16)) for h in hs}
            n *= 2
        xy = {h: _dot_nt(ar[h], s_old[h].astype(BF16)) + _dot(g_k[h], vb[h]) for h in hs}
        u_b = {h: _dot(inv[h].astype(BF16), xy[h][:C].astype(BF16)).astype(BF16) for h in hs}
        y = {h: xy[h][C:] + _dot(g_b[h][C:].astype(BF16), u_b[h]) for h in hs}
        for h in hs:
            sl = sls[h]
            upd = _dot_tn(jnp.concatenate([u_b[h], vb[h]], axis=0),
                          jnp.concatenate(bk[h], axis=0))
            s_scr[h] = (s_old[h] + upd) * dec[C - 1:C, sl]
            mean = jnp.mean(y[h], axis=-1, keepdims=True)
            yc = y[h] - mean
            var = jnp.mean(yc * yc, axis=-1, keepdims=True)
            yn = yc * lax.rsqrt(var + GN_EPS) * lnw_ref[:, sl] + lnb_ref[:, sl]
            bonus = jnp.sum(r[:, sl] * k[:, sl] * rk_ref[:, sl], axis=-1, keepdims=True) * v[:, sl]
            y_scr[:, sl] = yn + bonus

    o_ref[...] = (y_scr[...] * g).astype(o_ref.dtype)

    @pl.when(c_idx == pl.num_programs(1) - 1)
    def _():
        slast_ref[0] = s_scr[...]


def _rwkv_mixer(p, row0, bsz, t, shift0, s0, vfirst, prm, chunk):
    n_heads, head = s0.shape[1], s0.shape[2]
    rd = n_heads * head
    rw = prm["mu"].shape[1]
    has_vres = vfirst is not None
    nc = t // chunk
    blk0 = row0 // chunk
    np_ = p.shape[1]

    def full(a):
        return pl.BlockSpec(a.shape, lambda b, c: (0,) * a.ndim)

    ins = [p, shift0, s0]
    specs = [pl.BlockSpec((chunk, np_), lambda b, c: (blk0 + b * nc + c, 0)),
             pl.BlockSpec((1, 1, rw), lambda b, c: (b, 0, 0)),
             pl.BlockSpec((1, n_heads, head, head), lambda b, c: (b, 0, 0, 0))]
    if has_vres:
        ins.append(vfirst)
        specs.append(pl.BlockSpec((chunk, rd), lambda b, c: (b * nc + c, 0)))
    names = ["mu", "w0", "wdec", "a0", "wiclr", "wgate", "k_k", "k_a", "r_k", "lnx_w", "lnx_b"]
    if has_vres:
        names += ["v0", "vra", "vrb"]
    for nm in names:
        ins.append(prm[nm])
        specs.append(full(prm[nm]))

    row_spec = pl.BlockSpec((chunk, rd), lambda b, c: (b * nc + c, 0))
    outs = [jax.ShapeDtypeStruct((bsz * t, rd), BF16)]
    out_specs = [row_spec]
    if not has_vres:
        outs.append(jax.ShapeDtypeStruct((bsz * t, rd), F32))
        out_specs.append(row_spec)
    outs.append(jax.ShapeDtypeStruct(s0.shape, F32))
    out_specs.append(pl.BlockSpec((1, n_heads, head, head), lambda b, c: (b, 0, 0, 0)))

    res = pl.pallas_call(
        functools.partial(_rwkv_kernel, n_heads=n_heads, head=head, chunk=chunk, has_vres=has_vres,
                          head_group=n_heads),
        grid=(bsz, nc),
        in_specs=specs,
        out_specs=out_specs,
        out_shape=outs,
        scratch_shapes=[pltpu.VMEM((n_heads, head, head), F32),
                        pltpu.VMEM((1, rw), F32),
                        pltpu.VMEM((chunk, rd), F32)],
        compiler_params=_params("parallel", "arbitrary"),
        name="rwkv_mixer",
    )(*ins)
    if has_vres:
        o, s_last = res
        return o, vfirst, s_last
    o, v_out, s_last = res
    return o, v_out, s_last


def _rotate(x, cos, sin_a, sin_b):
    half = LANES // 4
    return x * cos + pltpu.roll(x, LANES - half, 1) * sin_a + pltpu.roll(x, half, 1) * sin_b


def _mla_proj_kernel(qa_ref, kv_ref, cos_ref, sina_ref, sinb_ref, gqa_ref, wqb_ref, gqn_ref, gqr_ref,
                     gkva_ref, gkr_ref, q_ref, rows_ref, c_ref, kr_ref, *, n_heads, rope, scale):
    cos, sin_a, sin_b = cos_ref[...], sina_ref[...], sinb_ref[...]
    hq = _rms(qa_ref[...], gqa_ref[...]).astype(BF16)
    q = _dot(hq, wqb_ref[...])
    hw = n_heads * LANES
    for h in range(n_heads):
        qn = _rms(q[:, h * LANES:(h + 1) * LANES], gqn_ref[...]) * scale
        qr = _rms(q[:, hw + h * LANES:hw + (h + 1) * LANES], gqr_ref[...], n=rope)
        qr = _rotate(qr, cos, sin_a, sin_b) * scale
        q_ref[:, 2 * h * LANES:(2 * h + 1) * LANES] = qn.astype(q_ref.dtype)
        q_ref[:, (2 * h + 1) * LANES:(2 * h + 2) * LANES] = qr.astype(q_ref.dtype)
    kv = kv_ref[...]
    lat = gkva_ref.shape[1]
    c = _rms(kv[:, :lat], gkva_ref[...])
    kr = _rotate(_rms(kv[:, lat:lat + LANES], gkr_ref[...], n=rope), cos, sin_a, sin_b)
    rows_ref[:, :lat] = c
    rows_ref[:, lat:] = kr[:, :rope]
    c_ref[...] = c.astype(c_ref.dtype)
    kr_ref[...] = kr.astype(kr_ref.dtype)


def _mla_proj(p, col_q, col_kv, tabs, prm, n_heads, rope, scale):
    m = p.shape[0]
    tm = _pick(m, (512, 256, 128, 64, 32, 16, 8))
    qw = prm["g_qa"].shape[1]
    lat = prm["g_kva"].shape[1]
    assert qw == 512 and col_q % qw == 0 and col_kv % qw == 0

    def full(a):
        return pl.BlockSpec(a.shape, lambda i: (0,) * a.ndim)

    tab_spec = pl.BlockSpec((tm, LANES), lambda i: (i, 0))
    ws = [prm["g_qa"], prm["w_qb"], prm["g_qn"], prm["g_qr"], prm["g_kva"], prm["g_kr"]]
    return pl.pallas_call(
        functools.partial(_mla_proj_kernel, n_heads=n_heads, rope=rope, scale=scale),
        grid=(m // tm,),
        in_specs=[pl.BlockSpec((tm, qw), lambda i: (i, col_q // qw)),
                  pl.BlockSpec((tm, qw), lambda i: (i, col_kv // qw)),
                  tab_spec, tab_spec, tab_spec] + [full(a) for a in ws],
        out_specs=[pl.BlockSpec((tm, 2 * n_heads * LANES), lambda i: (i, 0)),
                   pl.BlockSpec((tm, lat + rope), lambda i: (i, 0)),
                   pl.BlockSpec((tm, lat), lambda i: (i, 0)),
                   pl.BlockSpec((tm, LANES), lambda i: (i, 0))],
        out_shape=[jax.ShapeDtypeStruct((m, 2 * n_heads * LANES), BF16),
                   jax.ShapeDtypeStruct((m, lat + rope), F32),
                   jax.ShapeDtypeStruct((m, lat), BF16),
                   jax.ShapeDtypeStruct((m, LANES), BF16)],
        compiler_params=_params("parallel"),
        name="mla_proj",
    )(p, p, *tabs, *ws)


def _kv_expand_kernel(c_ref, kr_ref, wuk_ref, wuv_ref, gkn_ref, k_ref, v_ref, *, n_heads):
    c = c_ref[...]
    ke = _dot(c, wuk_ref[...])
    kr = kr_ref[...]
    for h in range(n_heads):
        kh = _rms(ke[:, h * LANES:(h + 1) * LANES], gkn_ref[...])
        k_ref[:, 2 * h * LANES:(2 * h + 1) * LANES] = kh.astype(k_ref.dtype)
        k_ref[:, (2 * h + 1) * LANES:(2 * h + 2) * LANES] = kr
    v_ref[...] = _dot_nt(wuv_ref[...], c).astype(v_ref.dtype)


def _kv_expand(c, kr, mp, wuk, wuv_t, gkn, n_heads):
    lat = c.shape[1]
    tm = _pick(mp, (512, 256, 128))
    assert wuk.shape[1] == n_heads * LANES and wuv_t.shape[0] == n_heads * LANES

    def full(a):
        return pl.BlockSpec(a.shape, lambda i: (0,) * a.ndim)

    return pl.pallas_call(
        functools.partial(_kv_expand_kernel, n_heads=n_heads),
        grid=(mp // tm,),
        in_specs=[pl.BlockSpec((tm, lat), lambda i: (i, 0)),
                  pl.BlockSpec((tm, LANES), lambda i: (i, 0)),
                  full(wuk), full(wuv_t), full(gkn)],
        out_specs=[pl.BlockSpec((tm, 2 * n_heads * LANES), lambda i: (i, 0)),
                   pl.BlockSpec((n_heads * LANES, tm), lambda i: (0, i))],
        out_shape=[jax.ShapeDtypeStruct((mp, 2 * n_heads * LANES), BF16),
                   jax.ShapeDtypeStruct((n_heads * LANES, mp), BF16)],
        compiler_params=_params("parallel"),
        name="kv_expand",
    )(c, kr, wuk, wuv_t, gkn)


def _flash_kernel(qi_ref, ki_ref, q_ref, k_ref, v_ref, o_ref, m_scr, l_scr, acc_scr, *, tq, tk):
    pair = pl.program_id(2)
    qi, ki = qi_ref[pair], ki_ref[pair]

    @pl.when(ki == 0)
    def _():
        m_scr[...] = jnp.full(m_scr.shape, -jnp.inf, F32)
        l_scr[...] = jnp.zeros(l_scr.shape, F32)
        acc_scr[...] = jnp.zeros(acc_scr.shape, F32)

    s = _dot_nt(k_ref[...], q_ref[...])
    key = ki * tk + lax.broadcasted_iota(jnp.int32, s.shape, 0)
    qry = qi * tq + lax.broadcasted_iota(jnp.int32, s.shape, 1)
    s = jnp.where(key <= qry, s, -jnp.inf)
    m_old = m_scr[...]
    m_new = jnp.maximum(m_old, jnp.max(s, axis=0, keepdims=True))
    alpha = jnp.exp(m_old - m_new)
    p = jnp.exp(s - m_new)
    l_scr[...] = alpha * l_scr[...] + jnp.sum(p, axis=0, keepdims=True)
    acc_scr[...] = alpha * acc_scr[...] + _dot(v_ref[...], p.astype(BF16))
    m_scr[...] = m_new

    @pl.when((ki + 1) * tk >= (qi + 1) * tq)
    def _():
        o_ref[...] = (acc_scr[...] / l_scr[...]).T.astype(o_ref.dtype)


def _flash_prompt(q, k, v, bsz, t, n_heads):
    tk = _pick(t, (512, 256, 128))
    tq = 2 * tk if t % (2 * tk) == 0 else tk
    nq, nk = t // tq, t // tk
    pairs = [(qi, ki) for qi in range(nq) for ki in range((qi + 1) * tq // tk)]
    qi_arr = jnp.asarray([pr[0] for pr in pairs], jnp.int32)
    ki_arr = jnp.asarray([pr[1] for pr in pairs], jnp.int32)
    grid_spec = pltpu.PrefetchScalarGridSpec(
        num_scalar_prefetch=2,
        grid=(bsz, n_heads, len(pairs)),
        in_specs=[pl.BlockSpec((tq, 2 * LANES), lambda b, h, pr, qa, ka: (b * nq + qa[pr], h)),
                  pl.BlockSpec((tk, 2 * LANES), lambda b, h, pr, qa, ka: (b * nk + ka[pr], h)),
                  pl.BlockSpec((LANES, tk), lambda b, h, pr, qa, ka: (h, b * nk + ka[pr]))],
        out_specs=pl.BlockSpec((tq, LANES), lambda b, h, pr, qa, ka: (b * nq + qa[pr], h)),
        scratch_shapes=[pltpu.VMEM((1, tq), F32), pltpu.VMEM((1, tq), F32), pltpu.VMEM((LANES, tq), F32)],
    )
    return pl.pallas_call(
        functools.partial(_flash_kernel, tq=tq, tk=tk),
        grid_spec=grid_spec,
        out_shape=jax.ShapeDtypeStruct((bsz * t, n_heads * LANES), BF16),
        compiler_params=_params("parallel", "parallel", "arbitrary"),
        name="flash_prompt",
    )(qi_arr, ki_arr, q, k, v)


def _sample_attn_kernel(pt_ref, q_ref, cn_ref, krn_ref, wuk3_ref, wukt_ref, wuv3_ref, gkn_ref, *rest,
                        n_heads, n_tok, n_page_in, sub_pages, lat, rope):
    page_refs = rest[:n_page_in]
    o_ref = rest[n_page_in]
    lhs_scr, qr_scr, m_scr, l_scr, acc_scr, s_scr, c_scr = rest[n_page_in + 1:]
    j = pl.program_id(1)
    nq = n_heads * n_tok
    hk = n_heads * LANES

    @pl.when(j == 0)
    def _():
        lhs_scr[:hk, :] = wukt_ref[...]
        for h in range(n_heads):
            qn = (q_ref[:, 2 * h * LANES:(2 * h + 1) * LANES].astype(F32) * gkn_ref[...]).astype(BF16)
            lhs_scr[hk + h * n_tok:hk + (h + 1) * n_tok, :] = _dot_nt(qn, wuk3_ref[h]).astype(BF16)
            qr_scr[h * n_tok:(h + 1) * n_tok, :] = q_ref[:, (2 * h + 1) * LANES:(2 * h + 2) * LANES]
        m_scr[...] = jnp.full(m_scr.shape, -jnp.inf, F32)
        l_scr[...] = jnp.zeros(l_scr.shape, F32)
        acc_scr[...] = jnp.zeros(acc_scr.shape, F32)
        s_scr[...] = jnp.full(s_scr.shape, NEUTRAL_SCORE, F32)
        c_scr[...] = jnp.zeros(c_scr.shape, BF16)

    def scores(big, s_rope):
        npos = big.shape[1]
        invs = []
        for h in range(n_heads):
            kh = big[h * LANES:(h + 1) * LANES]
            ms = jnp.sum(kh * kh, axis=0, keepdims=True) * (1.0 / LANES)
            invs.append(jnp.broadcast_to(lax.rsqrt(ms + RMS_EPS), (n_tok, npos)))
        return big[hk:] * jnp.concatenate(invs, axis=0) + s_rope

    diag = (lax.broadcasted_iota(jnp.int32, (nq, nq), 0) == lax.broadcasted_iota(jnp.int32, (nq, nq), 1))

    def as_row(col):
        return jnp.sum(jnp.where(diag, col, 0.0), axis=0, keepdims=True)

    def load_tile(sub):
        pages = [page_refs[sub * sub_pages + i][0, 0] for i in range(sub_pages)]
        c_t = jnp.concatenate([pg[:lat].astype(BF16) for pg in pages], axis=1)
        kr_t = jnp.concatenate([pg[lat:lat + rope].astype(BF16) for pg in pages], axis=1)
        return c_t, _dot(lhs_scr[...], c_t), _dot(qr_scr[:, :rope], kr_t)

    def softmax_update():
        s = s_scr[...]
        m_old = m_scr[...]
        m_new = jnp.maximum(m_old, jnp.max(s, axis=-1, keepdims=True))
        alpha = jnp.exp(m_old - m_new)
        p = jnp.exp(s - m_new)
        l_scr[...] = alpha * l_scr[...] + jnp.sum(p, axis=-1, keepdims=True)
        acc_scr[...] = as_row(alpha) * acc_scr[...] + _dot_nt(c_scr[...], p.astype(BF16))
        m_scr[...] = m_new

    n_sub = n_page_in // sub_pages
    nxt = load_tile(0)
    softmax_update()
    for sub in range(n_sub):
        c_t, big, s_rope = nxt
        if sub + 1 < n_sub:
            nxt = load_tile(sub + 1)
        npos = c_t.shape[1]
        s_scr[:, sub * npos:(sub + 1) * npos] = scores(big, s_rope)
        c_scr[:, sub * npos:(sub + 1) * npos] = c_t

    @pl.when(j == pl.num_programs(1) - 1)
    def _():
        softmax_update()
        c_n = cn_ref[...]
        row = lax.broadcasted_iota(jnp.int32, (nq, n_tok), 0)
        col = lax.broadcasted_iota(jnp.int32, (nq, n_tok), 1)
        s = scores(_dot_nt(lhs_scr[...], c_n), _dot_nt(qr_scr[:, :rope], krn_ref[:, :rope]))
        s = jnp.where(col <= (row % n_tok), s, -jnp.inf)
        m_old = m_scr[...]
        m_new = jnp.maximum(m_old, jnp.max(s, axis=-1, keepdims=True))
        alpha = jnp.exp(m_old - m_new)
        p = jnp.exp(s - m_new)
        l_fin = alpha * l_scr[...] + jnp.sum(p, axis=-1, keepdims=True)
        acc_fin = (as_row(alpha) * acc_scr[...]).T + _dot(p.astype(BF16), c_n)
        o_lat = (acc_fin / l_fin).astype(BF16)
        for h in range(n_heads):
            o_h = _dot(o_lat[h * n_tok:(h + 1) * n_tok], wuv3_ref[h])
            o_ref[:, h * LANES:(h + 1) * LANES] = o_h.astype(o_ref.dtype)


def _sample_attn(q, c_new, kr_new, row0, cache_t, layer, page_table, wuk3, wukt, wuv3, gkn,
                 n_tok, n_heads, rope):
    bs, n_pages = page_table.shape
    lat = c_new.shape[1]
    kvw, page = cache_t.shape[2], cache_t.shape[3]
    n_page_in = _pick(n_pages, (16, 8, 4, 2, 1))
    sub_pages = _pick(n_page_in, (4, 2, 1))
    steps = n_pages // n_page_in
    blk0 = row0 // n_tok
    nq = n_heads * n_tok

    def full(a):
        return pl.BlockSpec(a.shape, lambda b, j, pt: (0,) * a.ndim)

    def page_spec(i):
        return pl.BlockSpec((1, 1, kvw, page),
                            lambda b, j, pt: (pt[b * n_pages + j * n_page_in + i], layer, 0, 0))

    grid_spec = pltpu.PrefetchScalarGridSpec(
        num_scalar_prefetch=1,
        grid=(bs, steps),
        in_specs=[pl.BlockSpec((n_tok, 2 * n_heads * LANES), lambda b, j, pt: (blk0 + b, 0)),
                  pl.BlockSpec((n_tok, lat), lambda b, j, pt: (blk0 + b, 0)),
                  pl.BlockSpec((n_tok, LANES), lambda b, j, pt: (blk0 + b, 0)),
                  full(wuk3), full(wukt), full(wuv3), full(gkn)] + [page_spec(i) for i in range(n_page_in)],
        out_specs=pl.BlockSpec((n_tok, n_heads * LANES), lambda b, j, pt: (b, 0)),
        scratch_shapes=[pltpu.VMEM((n_heads * LANES + nq, lat), BF16), pltpu.VMEM((nq, LANES), BF16),
                        pltpu.VMEM((nq, 1), F32), pltpu.VMEM((nq, 1), F32), pltpu.VMEM((lat, nq), F32),
                        pltpu.VMEM((nq, n_page_in * page), F32), pltpu.VMEM((lat, n_page_in * page), BF16)],
    )
    return pl.pallas_call(
        functools.partial(_sample_attn_kernel, n_heads=n_heads, n_tok=n_tok, n_page_in=n_page_in,
                          sub_pages=sub_pages, lat=lat, rope=rope),
        grid_spec=grid_spec,
        out_shape=jax.ShapeDtypeStruct((bs * n_tok, n_heads * LANES), BF16),
        compiler_params=_params("parallel", "arbitrary"),
        name="sample_attn",
    )(page_table.reshape(-1), q, c_new, kr_new, wuk3, wukt, wuv3, gkn, *([cache_t] * n_page_in))


def _out_proj_kernel(x_ref, a_ref, b_ref, w_ref, o_ref):
    ka = a_ref.shape[1]
    o_ref[...] = x_ref[...] + _dot(a_ref[...], w_ref[:ka, :]) + _dot(b_ref[...], w_ref[ka:, :])


def _out_proj(x, a, b, w):
    m, d = x.shape
    tm = _pick(m, (1024, 512, 256, 128, 64, 32, 16, 8))
    tn = _pick(d, (512, 256, 128))
    return pl.pallas_call(
        _out_proj_kernel,
        grid=(m // tm, d // tn),
        in_specs=[pl.BlockSpec((tm, tn), lambda i, j: (i, j)),
                  pl.BlockSpec((tm, a.shape[1]), lambda i, j: (i, 0)),
                  pl.BlockSpec((tm, b.shape[1]), lambda i, j: (i, 0)),
                  pl.BlockSpec((w.shape[0], tn), lambda i, j: (0, j))],
        out_specs=pl.BlockSpec((tm, tn), lambda i, j: (i, j)),
        out_shape=jax.ShapeDtypeStruct((m, d), F32),
        compiler_params=_params("parallel", "parallel"),
        name="out_proj",
    )(x, a, b, w)


def _ffn_kernel(x_ref, g_ref, wg_ref, wu_ref, wd_ref, o_ref, hn_ref):
    @pl.when(pl.program_id(1) == 0)
    def _():
        x = x_ref[...]
        hn_ref[...] = _rms(x, g_ref[...]).astype(BF16)
        o_ref[...] = x

    h = hn_ref[...]
    gt = _dot(h, wg_ref[...])
    act = (gt * jax.nn.sigmoid(gt) * _dot(h, wu_ref[...])).astype(BF16)
    o_ref[...] += _dot(act, wd_ref[...])


def _ffn_dense(x, g, wg, wu, wd):
    m, d = x.shape
    f = wg.shape[1]
    tm = _pick(m, (512, 256, 128, 64, 32, 16, 8))
    tf = _pick(f, (512, 256, 128))
    return pl.pallas_call(
        _ffn_kernel,
        grid=(m // tm, f // tf),
        in_specs=[pl.BlockSpec((tm, d), lambda i, j: (i, 0)),
                  pl.BlockSpec((1, d), lambda i, j: (0, 0)),
                  pl.BlockSpec((d, tf), lambda i, j: (0, j)),
                  pl.BlockSpec((d, tf), lambda i, j: (0, j)),
                  pl.BlockSpec((tf, d), lambda i, j: (j, 0))],
        out_specs=pl.BlockSpec((tm, d), lambda i, j: (i, 0)),
        out_shape=jax.ShapeDtypeStruct((m, d), F32),
        scratch_shapes=[pltpu.VMEM((tm, d), BF16)],
        compiler_params=_params("parallel", "arbitrary"),
        name="ffn_dense",
    )(x, g, wg, wu, wd)


def _router_kernel(x_ref, g_ref, wr_ref, hn_ref, lg_ref):
    hn = _rms(x_ref[...], g_ref[...])
    hn_ref[...] = hn.astype(hn_ref.dtype)
    lg_ref[...] = _dot(hn, wr_ref[...], precision=lax.Precision.HIGHEST)


def _router(x, g, wr):
    m, d = x.shape
    tm = _pick(m, (512, 256, 128, 64, 32, 16, 8))
    return pl.pallas_call(
        _router_kernel,
        grid=(m // tm,),
        in_specs=[pl.BlockSpec((tm, d), lambda i: (i, 0)),
                  pl.BlockSpec((1, d), lambda i: (0, 0)),
                  pl.BlockSpec(wr.shape, lambda i: (0, 0))],
        out_specs=[pl.BlockSpec((tm, d), lambda i: (i, 0)),
                   pl.BlockSpec((tm, wr.shape[1]), lambda i: (i, 0))],
        out_shape=[jax.ShapeDtypeStruct((m, d), F32),
                   jax.ShapeDtypeStruct((m, wr.shape[1]), F32)],
        compiler_params=_params("parallel"),
        name="router",
    )(x, g, wr)


def _experts_kernel(blk_e_ref, n_used_ref, row_tok_ref, hn_hbm, gate_ref, wg_ref, wu_ref, wd_ref, o_ref,
                    rows_buf, xs_scr, sems, *, bm):
    i, j = pl.program_id(0), pl.program_id(1)
    n_used = n_used_ref[0]

    def block_copy(slot):
        return pltpu.make_async_copy(hn_hbm.at[pl.ds(0, bm)], rows_buf.at[slot], sems.at[slot])

    def start_gather(block, slot):
        def body(r, carry):
            tok = row_tok_ref[block * bm + r]
            pltpu.make_async_copy(hn_hbm.at[pl.ds(tok, 1)], rows_buf.at[slot, pl.ds(r, 1)],
                                  sems.at[slot]).start()
            return carry
        lax.fori_loop(0, bm, body, 0)

    @pl.when(j == 0)
    def _():
        o_ref[...] = jnp.zeros(o_ref.shape, o_ref.dtype)

        @pl.when(i == 0)
        def _():
            start_gather(0, 0)

        @pl.when(i < n_used)
        def _():
            slot = i % 2
            block_copy(slot).wait()
            xs_scr[...] = rows_buf[slot].astype(BF16)

        @pl.when(i + 1 < n_used)
        def _():
            start_gather(i + 1, (i + 1) % 2)

    @pl.when(i < n_used)
    def _():
        x = xs_scr[...]
        gt = _dot(x, wg_ref[0].astype(BF16))
        act = (gt * jax.nn.sigmoid(gt) * _dot(x, wu_ref[0].astype(BF16))).astype(BF16)
        o_ref[...] += _dot(act, wd_ref[0].astype(BF16))

    @pl.when(j == pl.num_programs(1) - 1)
    def _():
        o_ref[...] = o_ref[...] * gate_ref[...]


def _experts(hn, row_tok, row_gate, blk_e, n_used, wg, wu, wd, bm):
    n_rows = row_tok.shape[0]
    d = hn.shape[1]
    assert hn.shape[0] >= bm
    f = wg.shape[2]
    tf = _pick(f, (256, 128))
    grid_spec = pltpu.PrefetchScalarGridSpec(
        num_scalar_prefetch=3,
        grid=(n_rows // bm, f // tf),
        in_specs=[pl.BlockSpec(memory_space=pl.ANY),
                  pl.BlockSpec((bm, 1), lambda i, j, be, nu, rt: (i, 0)),
                  pl.BlockSpec((1, d, tf), lambda i, j, be, nu, rt: (be[i], 0, j)),
                  pl.BlockSpec((1, d, tf), lambda i, j, be, nu, rt: (be[i], 0, j)),
                  pl.BlockSpec((1, tf, d), lambda i, j, be, nu, rt: (be[i], j, 0))],
        out_specs=pl.BlockSpec((bm, d), lambda i, j, be, nu, rt: (i, 0)),
        scratch_shapes=[pltpu.VMEM((2, bm, d), F32), pltpu.VMEM((bm, d), BF16),
                        pltpu.SemaphoreType.DMA((2,))],
    )
    return pl.pallas_call(
        functools.partial(_experts_kernel, bm=bm),
        grid_spec=grid_spec,
        out_shape=jax.ShapeDtypeStruct((n_rows, d), F32),
        compiler_params=_params("arbitrary", "arbitrary"),
        name="experts",
    )(blk_e, n_used, row_tok, hn, row_gate, wg, wu, wd)


def _moe(x, g, wr, wg, wu, wd, splits):
    m, d = x.shape
    n_exp = wg.shape[0]
    wr_pad = jnp.pad(wr, ((0, 0), (0, LANES - n_exp)))
    hn, logits = _router(x, g, wr_pad)
    top_logit, top_e = lax.top_k(logits[:, :n_exp], TOP_K)
    gates = jax.nn.softmax(top_logit, axis=-1)
    n_assign = m * TOP_K
    bm = _pick(m, (1024, 512, 256, 128, 64, 32, 16, 8))
    flat_e = top_e.reshape(-1)
    onehot = (flat_e[:, None] == jnp.arange(n_exp, dtype=flat_e.dtype)[None, :]).astype(jnp.int32)
    csum = jnp.cumsum(onehot, axis=0)
    counts = csum[-1]
    rank = jnp.sum((csum - onehot) * onehot, axis=1)
    padded = (counts + bm - 1) // bm * bm
    pad_end = jnp.cumsum(padded)
    pad_start = pad_end - padded
    slot = pad_start[flat_e] + rank
    n_rows = _round_up(n_assign + n_exp * (bm - 1), bm)
    n_blocks = n_rows // bm
    flat_tok = jnp.repeat(jnp.arange(m, dtype=jnp.int32), TOP_K)
    assert m < 2 ** 24
    fill = jnp.zeros((n_rows, 2), F32)
    row_info = fill.at[slot].set(jnp.stack([flat_tok.astype(F32), gates.reshape(-1)], axis=1))
    row_tok = row_info[:, 0].astype(jnp.int32)
    row_gate = row_info[:, 1]
    blk_start = jnp.arange(n_blocks, dtype=jnp.int32) * bm
    blk_e = jnp.minimum(jnp.sum(pad_end[None, :] <= blk_start[:, None], axis=1), n_exp - 1).astype(jnp.int32)
    n_used = (pad_end[-1] // bm).astype(jnp.int32).reshape(1)
    yb = _experts(hn, row_tok, row_gate[:, None], blk_e, n_used, wg, wu, wd, bm)
    slot2 = slot.reshape(m, TOP_K)
    return [x[a:b] + yb[slot2[a:b, 0]] + yb[slot2[a:b, 1]] for a, b in splits]


def _pad_cols(a, width):
    return jnp.pad(a, [(0, 0)] * (a.ndim - 1) + [(0, width - a.shape[-1])])


def _rwkv_cols(a, rd, wl, al):
    o = 3 * rd
    return jnp.concatenate([a[..., :o], _pad_cols(a[..., o:o + wl], LANES),
                            _pad_cols(a[..., o + wl:o + wl + al], LANES),
                            _pad_cols(a[..., o + wl + al:], 2 * LANES)], axis=-1)


def _rwkv_cols_inv(a, rd, wl, al, gl):
    o = 3 * rd
    return jnp.concatenate([a[..., :o], a[..., o:o + wl], a[..., o + LANES:o + LANES + al],
                            a[..., o + 2 * LANES:o + 2 * LANES + gl]], axis=-1)


def _pad_rows(a, rows):
    return jnp.pad(a, ((0, rows - a.shape[0]), (0, 0)))


def _rope_tables(pos, rope):
    half = rope // 2
    assert half == LANES // 4
    inv_freq = ROPE_THETA ** (-jnp.arange(half, dtype=F32) * (2.0 / rope))
    ang = pos.astype(F32)[:, None] * inv_freq[None, :]
    cos, sin = jnp.cos(ang), jnp.sin(ang)
    z = jnp.zeros_like(cos)
    return (jnp.concatenate([cos, cos, z, z], axis=1),
            jnp.concatenate([-sin, z, z, z], axis=1),
            jnp.concatenate([z, sin, z, z], axis=1))


def kernel(x_prompt, x_sample, cache_kv, page_table, state_wkv, state_shift, g_mix, w_in, mu_shift, w0, w_decay, a0, w_iclr, w_gate_out, k_k, k_a, r_k, lnx_w, lnx_b, v0, v_res_a, v_res_b, g_qa, w_qb, g_qn, g_qr, g_kva, g_kr, w_uk, w_uv, g_kn, w_out, g_ffn, w_ffn_gate, w_ffn_up, w_ffn_down, w_router, w_exp_gate, w_exp_up, w_exp_down):
    bp, tp, d = x_prompt.shape
    bs, ts, _ = x_sample.shape
    depth = g_mix.shape[0]
    n_heads_r, head_r = r_k.shape[1], r_k.shape[2]
    rd = n_heads_r * head_r
    wl, al, gl = w_decay.shape[1], w_iclr.shape[1], w_gate_out.shape[1]
    rw_orig = 3 * rd + wl + al + gl
    ql, lat, rope, nope = g_qa.shape[1], g_kva.shape[1], g_kr.shape[1], g_qn.shape[1]
    n_heads_a, mla_v = w_uk.shape[2], w_uv.shape[3]
    assert nope == LANES and mla_v == LANES and rope == LANES // 2
    assert wl <= LANES and al <= LANES and gl <= 2 * LANES and v_res_a.shape[2] <= LANES
    past_len = page_table.shape[1] * PAGE_SIZE
    mp, ms = bp * tp, bs * ts
    scale = (nope + rope) ** -0.5
    rw = 3 * rd + 4 * LANES
    col_q, col_kv = rw, rw + ql
    chunk_p = _pick(tp, (64, 32, 16, 8))
    chunk_s = _pick(ts, (64, 32, 16, 8))

    x = jnp.concatenate([x_prompt.reshape(mp, d), x_sample.reshape(ms, d)], axis=0)
    pos = jnp.concatenate([jnp.tile(jnp.arange(tp, dtype=jnp.int32), bp),
                           jnp.tile(past_len + jnp.arange(ts, dtype=jnp.int32), bs)])
    tabs = _rope_tables(pos, rope)
    row2 = lambda a: a.reshape(1, -1)
    cache_t = jnp.transpose(cache_kv, (0, 1, 3, 2))

    shift0_p = jnp.zeros((bp, 1, rw), F32)
    wkv0_p = jnp.zeros((bp, n_heads_r, head_r, head_r), F32)
    vf_p = vf_s = None
    rows_p, rows_s, wkv_p, wkv_s, sh_p, sh_s = [], [], [], [], [], []
    for l in range(depth):
        w_in_l = w_in[l]
        w_pad = jnp.concatenate([
            _rwkv_cols(w_in_l[:, :rw_orig], rd, wl, al),
            w_in_l[:, rw_orig:rw_orig + ql + lat],
            _pad_cols(w_in_l[:, rw_orig + ql + lat:], 2 * LANES)], axis=1).astype(BF16)
        p = _in_proj(x, row2(g_mix[l]), w_pad)

        prm = {
            "mu": _rwkv_cols(row2(mu_shift[l]), rd, wl, al),
            "w0": row2(w0[l]), "a0": row2(a0[l]),
            "wdec": _pad_rows(w_decay[l], LANES).astype(BF16),
            "wiclr": _pad_rows(w_iclr[l], LANES).astype(BF16),
            "wgate": _pad_rows(w_gate_out[l], 2 * LANES).astype(BF16),
            "k_k": row2(k_k[l]), "k_a": row2(k_a[l]), "r_k": row2(r_k[l]),
            "lnx_w": row2(lnx_w[l]), "lnx_b": row2(lnx_b[l]),
        }
        if l > 0:
            prm["v0"] = row2(v0[l - 1])
            prm["vra"] = _pad_cols(v_res_a[l - 1], LANES).astype(BF16)
            prm["vrb"] = _pad_rows(v_res_b[l - 1], LANES).astype(BF16)
        oa_p, vf_p, wp = _rwkv_mixer(p, 0, bp, tp, shift0_p, wkv0_p, vf_p, prm, chunk_p)
        shift0_s = _rwkv_cols(state_shift[l], rd, wl, al)[:, None, :]
        oa_s, vf_s, ws = _rwkv_mixer(p, mp, bs, ts, shift0_s, state_wkv[l], vf_s, prm, chunk_s)

        wqb3 = w_qb[l].reshape(ql, n_heads_a, nope + rope)
        wqb_pad = jnp.concatenate([wqb3[:, :, :nope].reshape(ql, n_heads_a * nope),
                                   _pad_cols(wqb3[:, :, nope:], LANES).reshape(ql, n_heads_a * LANES)],
                                  axis=1).astype(BF16)
        mprm = {"g_qa": row2(g_qa[l]), "w_qb": wqb_pad, "g_qn": row2(g_qn[l]),
                "g_qr": _pad_cols(row2(g_qr[l]), LANES), "g_kva": row2(g_kva[l]),
                "g_kr": _pad_cols(row2(g_kr[l]), LANES)}
        q_cat, rows, c_bf, kr_bf = _mla_proj(p, col_q, col_kv, tabs, mprm, n_heads_a, rope, scale)

        wuk2 = w_uk[l].reshape(lat, n_heads_a * nope).astype(BF16)
        wuv_t = jnp.transpose(w_uv[l], (1, 2, 0)).reshape(n_heads_a * mla_v, lat).astype(BF16)
        gkn = row2(g_kn[l])
        k_cat, v_exp = _kv_expand(c_bf, kr_bf, mp, wuk2, wuv_t, gkn, n_heads_a)
        ob_p = _flash_prompt(q_cat, k_cat, v_exp, bp, tp, n_heads_a)

        wuk3 = jnp.transpose(w_uk[l], (1, 0, 2)).astype(BF16)
        wukt = jnp.transpose(w_uk[l], (1, 2, 0)).reshape(n_heads_a * nope, lat).astype(BF16)
        wuv3 = jnp.transpose(w_uv[l], (1, 0, 2)).astype(BF16)
        ob_s = _sample_attn(q_cat, c_bf, kr_bf, mp, cache_t, l, page_table, wuk3, wukt, wuv3, gkn,
                            ts, n_heads_a, rope)

        o_a = jnp.concatenate([oa_p, oa_s], axis=0)
        o_b = jnp.concatenate([ob_p, ob_s], axis=0)
        x = _out_proj(x, o_a, o_b, w_out[l].astype(BF16))

        i = l // 2
        if l % 2 == 0:
            x = _ffn_dense(x, row2(g_ffn[l]), w_ffn_gate[i].astype(BF16), w_ffn_up[i].astype(BF16),
                           w_ffn_down[i].astype(BF16))
        else:
            last = l == depth - 1
            parts = _moe(x, row2(g_ffn[l]), w_router[i], w_exp_gate[i], w_exp_up[i], w_exp_down[i],
                         [(0, mp), (mp, mp + ms)] if last else [(0, mp + ms)])
            x = None if last else parts[0]

        rows_p.append(rows[:mp].reshape(bp, tp, lat + rope))
        rows_s.append(rows[mp:].reshape(bs, ts, lat + rope))
        wkv_p.append(wp)
        wkv_s.append(ws)
        last_p = p[tp - 1:mp:tp, :rw]
        last_s = p[mp + ts - 1::ts, :rw]
        sh_p.append(_rwkv_cols_inv(last_p, rd, wl, al, gl))
        sh_s.append(_rwkv_cols_inv(last_s, rd, wl, al, gl))

    if x is not None:
        parts = [x[:mp], x[mp:]]
    return (parts[0].reshape(bp, tp, d), parts[1].reshape(bs, ts, d),
            jnp.stack(rows_p), jnp.stack(rows_s), jnp.stack(wkv_p), jnp.stack(wkv_s),
            jnp.stack(sh_p), jnp.stack(sh_s))
```

```python
import functools
import math

import jax
import jax.numpy as jnp
from jax import lax
from jax.experimental import pallas as pl
from jax.experimental.pallas import tpu as pltpu

F32 = jnp.float32
BF16 = jnp.bfloat16
RMS_EPS = 1e-6
GN_EPS = 64e-5
ROPE_THETA = 10000.0
TOP_K = 2
PAGE_SIZE = 128
LANES = 128
VMEM_LIMIT = 56 * 1024 * 1024
NEUTRAL_SCORE = -1e30


def _pick(n, prefs):
    for p in prefs:
        if n % p == 0:
            return p
    raise ValueError(f"no tile for {n} in {prefs}")


def _round_up(n, m):
    return (n + m - 1) // m * m


def _params(*sem):
    return pltpu.CompilerParams(dimension_semantics=sem, vmem_limit_bytes=VMEM_LIMIT)


def _dot(a, b, **kw):
    return jnp.dot(a, b, preferred_element_type=F32, **kw)


def _dot_nt(a, b):
    return lax.dot_general(a, b, (((1,), (1,)), ((), ())), preferred_element_type=F32)


def _dot_tn(a, b):
    return lax.dot_general(a, b, (((0,), (0,)), ((), ())), preferred_element_type=F32)


def _rms(x, g, n=None):
    n = x.shape[-1] if n is None else n
    ms = jnp.sum(x * x, axis=-1, keepdims=True) * (1.0 / n)
    return x * lax.rsqrt(ms + RMS_EPS) * g


def _in_proj_kernel(x_ref, g_ref, w_ref, o_ref, hn_ref):
    @pl.when(pl.program_id(1) == 0)
    def _():
        hn_ref[...] = _rms(x_ref[...], g_ref[...]).astype(BF16)

    o_ref[...] = _dot(hn_ref[...], w_ref[...])


def _in_proj(x, g, w):
    m, d = x.shape
    n = w.shape[1]
    tm = _pick(m, (1024, 512, 256, 128, 64, 32, 16, 8))
    tn = _pick(n, (1536, 512, 256, 128))
    return pl.pallas_call(
        _in_proj_kernel,
        grid=(m // tm, n // tn),
        in_specs=[pl.BlockSpec((tm, d), lambda i, j: (i, 0)),
                  pl.BlockSpec((1, d), lambda i, j: (0, 0)),
                  pl.BlockSpec((d, tn), lambda i, j: (0, j))],
        out_specs=pl.BlockSpec((tm, tn), lambda i, j: (i, j)),
        out_shape=jax.ShapeDtypeStruct((m, n), F32),
        scratch_shapes=[pltpu.VMEM((tm, d), BF16)],
        compiler_params=_params("parallel", "arbitrary"),
        name="in_proj",
    )(x, g, w)


def _rwkv_kernel(*refs, n_heads, head, chunk, has_vres, head_group):
    it = iter(refs)
    p_ref, shift0_ref, s0_ref = next(it), next(it), next(it)
    vfirst_ref = next(it) if has_vres else None
    mu_ref, w0_ref, wdec_ref, a0_ref, wiclr_ref, wgate_ref = (next(it) for _ in range(6))
    kk_ref, ka_ref, rk_ref, lnw_ref, lnb_ref = (next(it) for _ in range(5))
    if has_vres:
        v0_ref, vra_ref, vrb_ref = next(it), next(it), next(it)
    o_ref = next(it)
    vout_ref = None if has_vres else next(it)
    slast_ref = next(it)
    s_scr, carry_scr, y_scr = next(it), next(it), next(it)

    c_idx = pl.program_id(1)
    rd = n_heads * head
    rw = mu_ref.shape[1]
    C = chunk

    @pl.when(c_idx == 0)
    def _():
        s_scr[...] = s0_ref[0]
        carry_scr[...] = shift0_ref[0]

    p = p_ref[:, :rw]
    rid = lax.broadcasted_iota(jnp.int32, p.shape, 0)
    prev = jnp.where(rid == 0, carry_scr[...], pltpu.roll(p, 1, 0))
    carry_scr[...] = p[C - 1:C, :]
    m = p + (prev - p) * mu_ref[...]

    r = m[:, :rd]
    k = m[:, rd:2 * rd]
    v = m[:, 2 * rd:3 * rd]
    pw = m[:, 3 * rd:3 * rd + LANES]
    pa = m[:, 3 * rd + LANES:3 * rd + 2 * LANES]
    pg = m[:, 3 * rd + 2 * LANES:]

    u = w0_ref[...] + _dot(jnp.tanh(pw).astype(BF16), wdec_ref[...])
    lw = (-math.exp(-0.5)) * jax.nn.sigmoid(u)
    a = jax.nn.sigmoid(a0_ref[...] + _dot(pa.astype(BF16), wiclr_ref[...]))
    g = _dot(jax.nn.sigmoid(pg).astype(BF16), wgate_ref[...])
    if has_vres:
        lo = _dot(v.astype(BF16), vra_ref[...])
        v = v + (vfirst_ref[...] - v) * jax.nn.sigmoid(v0_ref[...] + _dot(lo.astype(BF16), vrb_ref[...]))
    else:
        vout_ref[...] = v
    kk_raw = k * kk_ref[...]
    k = k * (1.0 + (a - 1.0) * ka_ref[...])

    row = lax.broadcasted_iota(jnp.int32, (C, C), 0)
    col = lax.broadcasted_iota(jnp.int32, (C, C), 1)
    incl = col <= row
    strict = col < row
    cum = _dot(incl.astype(F32), lw, precision=lax.Precision.HIGHEST)
    dec = jnp.exp(cum)
    dec_prev = jnp.exp(cum - lw)
    dec_inv = jnp.exp(-cum)

    eye = (col == row).astype(F32)
    mask2 = jnp.concatenate([strict, incl], axis=0)

    for h0 in range(0, n_heads, head_group):
        hs = range(h0, min(h0 + head_group, n_heads))
        sls = {h: slice(h * head, (h + 1) * head) for h in hs}
        ar, bk, vb, s_old = {}, {}, {}, {}
        for h in hs:
            sl = sls[h]
            kk = kk_raw[:, sl]
            kk = kk / jnp.maximum(jnp.sqrt(jnp.sum(kk * kk, axis=-1, keepdims=True)), 1e-12)
            a_t = (-kk) * dec_prev[:, sl]
            r_t = r[:, sl] * dec[:, sl]
            b_t = (kk * a[:, sl]) * dec_inv[:, sl]
            k_t = k[:, sl] * dec_inv[:, sl]
            ar[h] = jnp.concatenate([a_t, r_t], axis=0).astype(BF16)
            bk[h] = (b_t.astype(BF16), k_t.astype(BF16))
            vb[h] = v[:, sl].astype(BF16)
            s_old[h] = s_scr[h]
        g_b = {h: jnp.where(mask2, _dot_nt(ar[h], bk[h][0]), 0.0) for h in hs}
        g_k = {h: jnp.where(mask2, _dot_nt(ar[h], bk[h][1]), 0.0).astype(BF16) for h in hs}
        pw_l = {h: g_b[h][:C] for h in hs}
        inv = {h: eye + pw_l[h] for h in hs}
        n = 1
        while 2 * n < C:
            pw_b = {h: pw_l[h].astype(BF16) for h in hs}
            pw_l = {h: _dot(pw_b[h], pw_b[h]) for h in hs}
            inv = {h: inv[h] + _dot(inv[h].astype(BF16), pw_l[h].astype(BF16)) for h in hs}
            n *= 2
        xy = {h: _dot_nt(ar[h], s_old[h].astype(BF16)) + _dot(g_k[h], vb[h]) for h in hs}
        u_b = {h: _dot(inv[h].astype(BF16), xy[h][:C].astype(BF16)).astype(BF16) for h in hs}
        y = {h: xy[h][C:] + _dot(g_b[h][C:].astype(BF16), u_b[h]) for h in hs}
        for h in hs:
            sl = sls[h]
            upd = _dot_tn(jnp.concatenate([u_b[h], vb[h]], axis=0),
                          jnp.concatenate(bk[h], axis=0))
            s_scr[h] = (s_old[h] + upd) * dec[C - 1:C, sl]
            mean = jnp.mean(y[h], axis=-1, keepdims=True)
            yc = y[h] - mean
            var = jnp.mean(yc * yc, axis=-1, keepdims=True)
            yn = yc * lax.rsqrt(var + GN_EPS) * lnw_ref[:, sl] + lnb_ref[:, sl]
            bonus = jnp.sum(r[:, sl] * k[:, sl] * rk_ref[:, sl], axis=-1, keepdims=True) * v[:, sl]
            y_scr[:, sl] = yn + bonus

    o_ref[...] = (y_scr[...] * g).astype(o_ref.dtype)

    @pl.when(c_idx == pl.num_programs(1) - 1)
    def _():
        slast_ref[0] = s_scr[...]


def _rwkv_mixer(p, row0, bsz, t, shift0, s0, vfirst, prm, chunk):
    n_heads, head = s0.shape[1], s0.shape[2]
    rd = n_heads * head
    rw = prm["mu"].shape[1]
    has_vres = vfirst is not None
    nc = t // chunk
    blk0 = row0 // chunk
    np_ = p.shape[1]

    def full(a):
        return pl.BlockSpec(a.shape, lambda b, c: (0,) * a.ndim)

    ins = [p, shift0, s0]
    specs = [pl.BlockSpec((chunk, np_), lambda b, c: (blk0 + b * nc + c, 0)),
             pl.BlockSpec((1, 1, rw), lambda b, c: (b, 0, 0)),
             pl.BlockSpec((1, n_heads, head, head), lambda b, c: (b, 0, 0, 0))]
    if has_vres:
        ins.append(vfirst)
        specs.append(pl.BlockSpec((chunk, rd), lambda b, c: (b * nc + c, 0)))
    names = ["mu", "w0", "wdec", "a0", "wiclr", "wgate", "k_k", "k_a", "r_k", "lnx_w", "lnx_b"]
    if has_vres:
        names += ["v0", "vra", "vrb"]
    for nm in names:
        ins.append(prm[nm])
        specs.append(full(prm[nm]))

    row_spec = pl.BlockSpec((chunk, rd), lambda b, c: (b * nc + c, 0))
    outs = [jax.ShapeDtypeStruct((bsz * t, rd), BF16)]
    out_specs = [row_spec]
    if not has_vres:
        outs.append(jax.ShapeDtypeStruct((bsz * t, rd), F32))
        out_specs.append(row_spec)
    outs.append(jax.ShapeDtypeStruct(s0.shape, F32))
    out_specs.append(pl.BlockSpec((1, n_heads, head, head), lambda b, c: (b, 0, 0, 0)))

    res = pl.pallas_call(
        functools.partial(_rwkv_kernel, n_heads=n_heads, head=head, chunk=chunk, has_vres=has_vres,
                          head_group=n_heads),
        grid=(bsz, nc),
        in_specs=specs,
        out_specs=out_specs,
        out_shape=outs,
        scratch_shapes=[pltpu.VMEM((n_heads, head, head), F32),
                        pltpu.VMEM((1, rw), F32),
                        pltpu.VMEM((chunk, rd), F32)],
        compiler_params=_params("parallel", "arbitrary"),
        name="rwkv_mixer",
    )(*ins)
    if has_vres:
        o, s_last = res
        return o, vfirst, s_last
    o, v_out, s_last = res
    return o, v_out, s_last


def _rotate(x, cos, sin_a, sin_b):
    half = LANES // 4
    return x * cos + pltpu.roll(x, LANES - half, 1) * sin_a + pltpu.roll(x, half, 1) * sin_b


def _mla_proj_kernel(qa_ref, kv_ref, cos_ref, sina_ref, sinb_ref, gqa_ref, wqb_ref, gqn_ref, gqr_ref,
                     gkva_ref, gkr_ref, q_ref, rows_ref, c_ref, kr_ref, *, n_heads, rope, scale):
    cos, sin_a, sin_b = cos_ref[...], sina_ref[...], sinb_ref[...]
    hq = _rms(qa_ref[...], gqa_ref[...]).astype(BF16)
    q = _dot(hq, wqb_ref[...])
    hw = n_heads * LANES
    for h in range(n_heads):
        qn = _rms(q[:, h * LANES:(h + 1) * LANES], gqn_ref[...]) * scale
        qr = _rms(q[:, hw + h * LANES:hw + (h + 1) * LANES], gqr_ref[...], n=rope)
        qr = _rotate(qr, cos, sin_a, sin_b) * scale
        q_ref[:, 2 * h * LANES:(2 * h + 1) * LANES] = qn.astype(q_ref.dtype)
        q_ref[:, (2 * h + 1) * LANES:(2 * h + 2) * LANES] = qr.astype(q_ref.dtype)
    kv = kv_ref[...]
    lat = gkva_ref.shape[1]
    c = _rms(kv[:, :lat], gkva_ref[...])
    kr = _rotate(_rms(kv[:, lat:lat + LANES], gkr_ref[...], n=rope), cos, sin_a, sin_b)
    rows_ref[:, :lat] = c
    rows_ref[:, lat:] = kr[:, :rope]
    c_ref[...] = c.astype(c_ref.dtype)
    kr_ref[...] = kr.astype(kr_ref.dtype)


def _mla_proj(p, col_q, col_kv, tabs, prm, n_heads, rope, scale):
    m = p.shape[0]
    tm = _pick(m, (512, 256, 128, 64, 32, 16, 8))
    qw = prm["g_qa"].shape[1]
    lat = prm["g_kva"].shape[1]
    assert qw == 512 and col_q % qw == 0 and col_kv % qw == 0

    def full(a):
        return pl.BlockSpec(a.shape, lambda i: (0,) * a.ndim)

    tab_spec = pl.BlockSpec((tm, LANES), lambda i: (i, 0))
    ws = [prm["g_qa"], prm["w_qb"], prm["g_qn"], prm["g_qr"], prm["g_kva"], prm["g_kr"]]
    return pl.pallas_call(
        functools.partial(_mla_proj_kernel, n_heads=n_heads, rope=rope, scale=scale),
        grid=(m // tm,),
        in_specs=[pl.BlockSpec((tm, qw), lambda i: (i, col_q // qw)),
                  pl.BlockSpec((tm, qw), lambda i: (i, col_kv // qw)),
                  tab_spec, tab_spec, tab_spec] + [full(a) for a in ws],
        out_specs=[pl.BlockSpec((tm, 2 * n_heads * LANES), lambda i: (i, 0)),
                   pl.BlockSpec((tm, lat + rope), lambda i: (i, 0)),
                   pl.BlockSpec((tm, lat), lambda i: (i, 0)),
                   pl.BlockSpec((tm, LANES), lambda i: (i, 0))],
        out_shape=[jax.ShapeDtypeStruct((m, 2 * n_heads * LANES), BF16),
                   jax.ShapeDtypeStruct((m, lat + rope), F32),
                   jax.ShapeDtypeStruct((m, lat), BF16),
                   jax.ShapeDtypeStruct((m, LANES), BF16)],
        compiler_params=_params("parallel"),
        name="mla_proj",
    )(p, p, *tabs, *ws)


def _kv_expand_kernel(c_ref, kr_ref, wuk_ref, wuv_ref, gkn_ref, k_ref, v_ref, *, n_heads):
    c = c_ref[...]
    ke = _dot(c, wuk_ref[...])
    kr = kr_ref[...]
    for h in range(n_heads):
        kh = _rms(ke[:, h * LANES:(h + 1) * LANES], gkn_ref[...])
        k_ref[:, 2 * h * LANES:(2 * h + 1) * LANES] = kh.astype(k_ref.dtype)
        k_ref[:, (2 * h + 1) * LANES:(2 * h + 2) * LANES] = kr
    v_ref[...] = _dot_nt(wuv_ref[...], c).astype(v_ref.dtype)


def _kv_expand(c, kr, mp, wuk, wuv_t, gkn, n_heads):
    lat = c.shape[1]
    tm = _pick(mp, (512, 256, 128))
    assert wuk.shape[1] == n_heads * LANES and wuv_t.shape[0] == n_heads * LANES

    def full(a):
        return pl.BlockSpec(a.shape, lambda i: (0,) * a.ndim)

    return pl.pallas_call(
        functools.partial(_kv_expand_kernel, n_heads=n_heads),
        grid=(mp // tm,),
        in_specs=[pl.BlockSpec((tm, lat), lambda i: (i, 0)),
                  pl.BlockSpec((tm, LANES), lambda i: (i, 0)),
                  full(wuk), full(wuv_t), full(gkn)],
        out_specs=[pl.BlockSpec((tm, 2 * n_heads * LANES), lambda i: (i, 0)),
                   pl.BlockSpec((n_heads * LANES, tm), lambda i: (0, i))],
        out_shape=[jax.ShapeDtypeStruct((mp, 2 * n_heads * LANES), BF16),
                   jax.ShapeDtypeStruct((n_heads * LANES, mp), BF16)],
        compiler_params=_params("parallel"),
        name="kv_expand",
    )(c, kr, wuk, wuv_t, gkn)


def _flash_kernel(qi_ref, ki_ref, q_ref, k_ref, v_ref, o_ref, m_scr, l_scr, acc_scr, *, tq, tk):
    pair = pl.program_id(2)
    qi, ki = qi_ref[pair], ki_ref[pair]

    @pl.when(ki == 0)
    def _():
        m_scr[...] = jnp.full(m_scr.shape, -jnp.inf, F32)
        l_scr[...] = jnp.zeros(l_scr.shape, F32)
        acc_scr[...] = jnp.zeros(acc_scr.shape, F32)

    s = _dot_nt(k_ref[...], q_ref[...])
    key = ki * tk + lax.broadcasted_iota(jnp.int32, s.shape, 0)
    qry = qi * tq + lax.broadcasted_iota(jnp.int32, s.shape, 1)
    s = jnp.where(key <= qry, s, -jnp.inf)
    m_old = m_scr[...]
    m_new = jnp.maximum(m_old, jnp.max(s, axis=0, keepdims=True))
    alpha = jnp.exp(m_old - m_new)
    p = jnp.exp(s - m_new)
    l_scr[...] = alpha * l_scr[...] + jnp.sum(p, axis=0, keepdims=True)
    acc_scr[...] = alpha * acc_scr[...] + _dot(v_ref[...], p.astype(BF16))
    m_scr[...] = m_new

    @pl.when((ki + 1) * tk >= (qi + 1) * tq)
    def _():
        o_ref[...] = (acc_scr[...] / l_scr[...]).T.astype(o_ref.dtype)


def _flash_prompt(q, k, v, bsz, t, n_heads):
    tk = _pick(t, (1024, 512, 256, 128))
    tq = tk if tk >= 1024 else (2 * tk if t % (2 * tk) == 0 else tk)
    nq, nk = t // tq, t // tk
    pairs = [(qi, ki) for qi in range(nq) for ki in range((qi + 1) * tq // tk)]
    qi_arr = jnp.asarray([pr[0] for pr in pairs], jnp.int32)
    ki_arr = jnp.asarray([pr[1] for pr in pairs], jnp.int32)
    grid_spec = pltpu.PrefetchScalarGridSpec(
        num_scalar_prefetch=2,
        grid=(bsz, n_heads, len(pairs)),
        in_specs=[pl.BlockSpec((tq, 2 * LANES), lambda b, h, pr, qa, ka: (b * nq + qa[pr], h)),
                  pl.BlockSpec((tk, 2 * LANES), lambda b, h, pr, qa, ka: (b * nk + ka[pr], h)),
                  pl.BlockSpec((LANES, tk), lambda b, h, pr, qa, ka: (h, b * nk + ka[pr]))],
        out_specs=pl.BlockSpec((tq, LANES), lambda b, h, pr, qa, ka: (b * nq + qa[pr], h)),
        scratch_shapes=[pltpu.VMEM((1, tq), F32), pltpu.VMEM((1, tq), F32), pltpu.VMEM((LANES, tq), F32)],
    )
    return pl.pallas_call(
        functools.partial(_flash_kernel, tq=tq, tk=tk),
        grid_spec=grid_spec,
        out_shape=jax.ShapeDtypeStruct((bsz * t, n_heads * LANES), BF16),
        compiler_params=_params("parallel", "parallel", "arbitrary"),
        name="flash_prompt",
    )(qi_arr, ki_arr, q, k, v)


def _sample_attn_kernel(pt_ref, q_ref, cn_ref, krn_ref, wuk3_ref, wukt_ref, wuv3_ref, gkn_ref, *rest,
                        n_heads, n_tok, n_page_in, sub_pages, lat, rope):
    page_refs = rest[:n_page_in]
    o_ref = rest[n_page_in]
    lhs_scr, qr_scr, m_scr, l_scr, acc_scr, s_scr, c_scr = rest[n_page_in + 1:]
    j = pl.program_id(1)
    nq = n_heads * n_tok
    hk = n_heads * LANES

    @pl.when(j == 0)
    def _():
        lhs_scr[:hk, :] = wukt_ref[...]
        for h in range(n_heads):
            qn = (q_ref[:, 2 * h * LANES:(2 * h + 1) * LANES].astype(F32) * gkn_ref[...]).astype(BF16)
            lhs_scr[hk + h * n_tok:hk + (h + 1) * n_tok, :] = _dot_nt(qn, wuk3_ref[h]).astype(BF16)
            qr_scr[h * n_tok:(h + 1) * n_tok, :] = q_ref[:, (2 * h + 1) * LANES:(2 * h + 2) * LANES]
        m_scr[...] = jnp.full(m_scr.shape, -jnp.inf, F32)
        l_scr[...] = jnp.zeros(l_scr.shape, F32)
        acc_scr[...] = jnp.zeros(acc_scr.shape, F32)
        s_scr[...] = jnp.full(s_scr.shape, NEUTRAL_SCORE, F32)
        c_scr[...] = jnp.zeros(c_scr.shape, BF16)

    def scores(big, s_rope):
        npos = big.shape[1]
        invs = []
        for h in range(n_heads):
            kh = big[h * LANES:(h + 1) * LANES]
            ms = jnp.sum(kh * kh, axis=0, keepdims=True) * (1.0 / LANES)
            invs.append(jnp.broadcast_to(lax.rsqrt(ms + RMS_EPS), (n_tok, npos)))
        return big[hk:] * jnp.concatenate(invs, axis=0) + s_rope

    diag = (lax.broadcasted_iota(jnp.int32, (nq, nq), 0) == lax.broadcasted_iota(jnp.int32, (nq, nq), 1))

    def as_row(col):
        return jnp.sum(jnp.where(diag, col, 0.0), axis=0, keepdims=True)

    def load_tile(sub):
        pages = [page_refs[sub * sub_pages + i][0, 0] for i in range(sub_pages)]
        c_t = jnp.concatenate([pg[:lat].astype(BF16) for pg in pages], axis=1)
        kr_t = jnp.concatenate([pg[lat:lat + rope].astype(BF16) for pg in pages], axis=1)
        return c_t, _dot(lhs_scr[...], c_t), _dot(qr_scr[:, :rope], kr_t)

    def softmax_update():
        s = s_scr[...]
        m_old = m_scr[...]
        m_new = jnp.maximum(m_old, jnp.max(s, axis=-1, keepdims=True))
        alpha = jnp.exp(m_old - m_new)
        p = jnp.exp(s - m_new)
        l_scr[...] = alpha * l_scr[...] + jnp.sum(p, axis=-1, keepdims=True)
        acc_scr[...] = as_row(alpha) * acc_scr[...] + _dot_nt(c_scr[...], p.astype(BF16))
        m_scr[...] = m_new

    n_sub = n_page_in // sub_pages
    nxt = load_tile(0)
    softmax_update()
    for sub in range(n_sub):
        c_t, big, s_rope = nxt
        if sub + 1 < n_sub:
            nxt = load_tile(sub + 1)
        npos = c_t.shape[1]
        s_scr[:, sub * npos:(sub + 1) * npos] = scores(big, s_rope)
        c_scr[:, sub * npos:(sub + 1) * npos] = c_t

    @pl.when(j == pl.num_programs(1) - 1)
    def _():
        softmax_update()
        c_n = cn_ref[...]
        row = lax.broadcasted_iota(jnp.int32, (nq, n_tok), 0)
        col = lax.broadcasted_iota(jnp.int32, (nq, n_tok), 1)
        s = scores(_dot_nt(lhs_scr[...], c_n), _dot_nt(qr_scr[:, :rope], krn_ref[:, :rope]))
        s = jnp.where(col <= (row % n_tok), s, -jnp.inf)
        m_old = m_scr[...]
        m_new = jnp.maximum(m_old, jnp.max(s, axis=-1, keepdims=True))
        alpha = jnp.exp(m_old - m_new)
        p = jnp.exp(s - m_new)
        l_fin = alpha * l_scr[...] + jnp.sum(p, axis=-1, keepdims=True)
        acc_fin = (as_row(alpha) * acc_scr[...]).T + _dot(p.astype(BF16), c_n)
        o_lat = (acc_fin / l_fin).astype(BF16)
        for h in range(n_heads):
            o_h = _dot(o_lat[h * n_tok:(h + 1) * n_tok], wuv3_ref[h])
            o_ref[:, h * LANES:(h + 1) * LANES] = o_h.astype(o_ref.dtype)


def _sample_attn(q, c_new, kr_new, row0, cache_t, layer, page_table, wuk3, wukt, wuv3, gkn,
                 n_tok, n_heads, rope):
    bs, n_pages = page_table.shape
    lat = c_new.shape[1]
    kvw, page = cache_t.shape[2], cache_t.shape[3]
    n_page_in = _pick(n_pages, (16, 8, 4, 2, 1))
    sub_pages = _pick(n_page_in, (4, 2, 1))
    steps = n_pages // n_page_in
    blk0 = row0 // n_tok
    nq = n_heads * n_tok

    def full(a):
        return pl.BlockSpec(a.shape, lambda b, j, pt: (0,) * a.ndim)

    def page_spec(i):
        return pl.BlockSpec((1, 1, kvw, page),
                            lambda b, j, pt: (pt[b * n_pages + j * n_page_in + i], layer, 0, 0))

    grid_spec = pltpu.PrefetchScalarGridSpec(
        num_scalar_prefetch=1,
        grid=(bs, steps),
        in_specs=[pl.BlockSpec((n_tok, 2 * n_heads * LANES), lambda b, j, pt: (blk0 + b, 0)),
                  pl.BlockSpec((n_tok, lat), lambda b, j, pt: (blk0 + b, 0)),
                  pl.BlockSpec((n_tok, LANES), lambda b, j, pt: (blk0 + b, 0)),
                  full(wuk3), full(wukt), full(wuv3), full(gkn)] + [page_spec(i) for i in range(n_page_in)],
        out_specs=pl.BlockSpec((n_tok, n_heads * LANES), lambda b, j, pt: (b, 0)),
        scratch_shapes=[pltpu.VMEM((n_heads * LANES + nq, lat), BF16), pltpu.VMEM((nq, LANES), BF16),
                        pltpu.VMEM((nq, 1), F32), pltpu.VMEM((nq, 1), F32), pltpu.VMEM((lat, nq), F32),
                        pltpu.VMEM((nq, n_page_in * page), F32), pltpu.VMEM((lat, n_page_in * page), BF16)],
    )
    return pl.pallas_call(
        functools.partial(_sample_attn_kernel, n_heads=n_heads, n_tok=n_tok, n_page_in=n_page_in,
                          sub_pages=sub_pages, lat=lat, rope=rope),
        grid_spec=grid_spec,
        out_shape=jax.ShapeDtypeStruct((bs * n_tok, n_heads * LANES), BF16),
        compiler_params=_params("parallel", "arbitrary"),
        name="sample_attn",
    )(page_table.reshape(-1), q, c_new, kr_new, wuk3, wukt, wuv3, gkn, *([cache_t] * n_page_in))


def _out_proj_kernel(x_ref, a_ref, b_ref, w_ref, o_ref):
    ka = a_ref.shape[1]
    o_ref[...] = x_ref[...] + _dot(a_ref[...], w_ref[:ka, :]) + _dot(b_ref[...], w_ref[ka:, :])


def _out_proj(x, a, b, w):
    m, d = x.shape
    tm = _pick(m, (1024, 512, 256, 128, 64, 32, 16, 8))
    tn = _pick(d, (512, 256, 128))
    return pl.pallas_call(
        _out_proj_kernel,
        grid=(m // tm, d // tn),
        in_specs=[pl.BlockSpec((tm, tn), lambda i, j: (i, j)),
                  pl.BlockSpec((tm, a.shape[1]), lambda i, j: (i, 0)),
                  pl.BlockSpec((tm, b.shape[1]), lambda i, j: (i, 0)),
                  pl.BlockSpec((w.shape[0], tn), lambda i, j: (0, j))],
        out_specs=pl.BlockSpec((tm, tn), lambda i, j: (i, j)),
        out_shape=jax.ShapeDtypeStruct((m, d), F32),
        compiler_params=_params("parallel", "parallel"),
        name="out_proj",
    )(x, a, b, w)


def _ffn_kernel(x_ref, g_ref, wg_ref, wu_ref, wd_ref, o_ref, hn_ref):
    @pl.when(pl.program_id(1) == 0)
    def _():
        x = x_ref[...]
        hn_ref[...] = _rms(x, g_ref[...]).astype(BF16)
        o_ref[...] = x

    h = hn_ref[...]
    gt = _dot(h, wg_ref[...])
    act = (gt * jax.nn.sigmoid(gt) * _dot(h, wu_ref[...])).astype(BF16)
    o_ref[...] += _dot(act, wd_ref[...])


def _ffn_dense(x, g, wg, wu, wd):
    m, d = x.shape
    f = wg.shape[1]
    tm = _pick(m, (512, 256, 128, 64, 32, 16, 8))
    tf = _pick(f, (512, 256, 128))
    return pl.pallas_call(
        _ffn_kernel,
        grid=(m // tm, f // tf),
        in_specs=[pl.BlockSpec((tm, d), lambda i, j: (i, 0)),
                  pl.BlockSpec((1, d), lambda i, j: (0, 0)),
                  pl.BlockSpec((d, tf), lambda i, j: (0, j)),
                  pl.BlockSpec((d, tf), lambda i, j: (0, j)),
                  pl.BlockSpec((tf, d), lambda i, j: (j, 0))],
        out_specs=pl.BlockSpec((tm, d), lambda i, j: (i, 0)),
        out_shape=jax.ShapeDtypeStruct((m, d), F32),
        scratch_shapes=[pltpu.VMEM((tm, d), BF16)],
        compiler_params=_params("parallel", "arbitrary"),
        name="ffn_dense",
    )(x, g, wg, wu, wd)


def _router_kernel(x_ref, g_ref, wr_ref, hn_ref, lg_ref):
    hn = _rms(x_ref[...], g_ref[...])
    hn_ref[...] = hn.astype(hn_ref.dtype)
    lg_ref[...] = _dot(hn, wr_ref[...], precision=lax.Precision.HIGHEST)


def _router(x, g, wr):
    m, d = x.shape
    tm = _pick(m, (512, 256, 128, 64, 32, 16, 8))
    return pl.pallas_call(
        _router_kernel,
        grid=(m // tm,),
        in_specs=[pl.BlockSpec((tm, d), lambda i: (i, 0)),
                  pl.BlockSpec((1, d), lambda i: (0, 0)),
                  pl.BlockSpec(wr.shape, lambda i: (0, 0))],
        out_specs=[pl.BlockSpec((tm, d), lambda i: (i, 0)),
                   pl.BlockSpec((tm, wr.shape[1]), lambda i: (i, 0))],
        out_shape=[jax.ShapeDtypeStruct((m, d), F32),
                   jax.ShapeDtypeStruct((m, wr.shape[1]), F32)],
        compiler_params=_params("parallel"),
        name="router",
    )(x, g, wr)


def _experts_kernel(blk_e_ref, n_used_ref, row_tok_ref, hn_hbm, gate_ref, wg_ref, wu_ref, wd_ref, o_ref,
                    rows_buf, xs_scr, sems, *, bm):
    i, j = pl.program_id(0), pl.program_id(1)
    n_used = n_used_ref[0]

    def block_copy(slot):
        return pltpu.make_async_copy(hn_hbm.at[pl.ds(0, bm)], rows_buf.at[slot], sems.at[slot])

    def start_gather(block, slot):
        def body(r, carry):
            tok = row_tok_ref[block * bm + r]
            pltpu.make_async_copy(hn_hbm.at[pl.ds(tok, 1)], rows_buf.at[slot, pl.ds(r, 1)],
                                  sems.at[slot]).start()
            return carry
        lax.fori_loop(0, bm, body, 0, unroll=8)

    @pl.when(j == 0)
    def _():
        o_ref[...] = jnp.zeros(o_ref.shape, o_ref.dtype)

        @pl.when(i == 0)
        def _():
            start_gather(0, 0)

        @pl.when(i < n_used)
        def _():
            slot = i % 2
            block_copy(slot).wait()
            xs_scr[...] = rows_buf[slot].astype(BF16)

        @pl.when(i + 1 < n_used)
        def _():
            start_gather(i + 1, (i + 1) % 2)

    @pl.when(i < n_used)
    def _():
        x = xs_scr[...]
        gt = _dot(x, wg_ref[0].astype(BF16))
        act = (gt * jax.nn.sigmoid(gt) * _dot(x, wu_ref[0].astype(BF16))).astype(BF16)
        o_ref[...] += _dot(act, wd_ref[0].astype(BF16))

    @pl.when(j == pl.num_programs(1) - 1)
    def _():
        o_ref[...] = o_ref[...] * gate_ref[...]


def _experts(hn, row_tok, row_gate, blk_e, n_used, wg, wu, wd, bm):
    n_rows = row_tok.shape[0]
    d = hn.shape[1]
    assert hn.shape[0] >= bm
    f = wg.shape[2]
    tf = _pick(f, (256, 128))
    grid_spec = pltpu.PrefetchScalarGridSpec(
        num_scalar_prefetch=3,
        grid=(n_rows // bm, f // tf),
        in_specs=[pl.BlockSpec(memory_space=pl.ANY),
                  pl.BlockSpec((bm, 1), lambda i, j, be, nu, rt: (i, 0)),
                  pl.BlockSpec((1, d, tf), lambda i, j, be, nu, rt: (be[i], 0, j)),
                  pl.BlockSpec((1, d, tf), lambda i, j, be, nu, rt: (be[i], 0, j)),
                  pl.BlockSpec((1, tf, d), lambda i, j, be, nu, rt: (be[i], j, 0))],
        out_specs=pl.BlockSpec((bm, d), lambda i, j, be, nu, rt: (i, 0)),
        scratch_shapes=[pltpu.VMEM((2, bm, d), F32), pltpu.VMEM((bm, d), BF16),
                        pltpu.SemaphoreType.DMA((2,))],
    )
    return pl.pallas_call(
        functools.partial(_experts_kernel, bm=bm),
        grid_spec=grid_spec,
        out_shape=jax.ShapeDtypeStruct((n_rows, d), F32),
        compiler_params=_params("arbitrary", "arbitrary"),
        name="experts",
    )(blk_e, n_used, row_tok, hn, row_gate, wg, wu, wd)


def _moe(x, g, wr, wg, wu, wd, splits):
    m, d = x.shape
    n_exp = wg.shape[0]
    wr_pad = jnp.pad(wr, ((0, 0), (0, LANES - n_exp)))
    hn, logits = _router(x, g, wr_pad)
    top_logit, top_e = lax.top_k(logits[:, :n_exp], TOP_K)
    gates = jax.nn.softmax(top_logit, axis=-1)
    n_assign = m * TOP_K
    bm = _pick(m, (1024, 512, 256, 128, 64, 32, 16, 8))
    flat_e = top_e.reshape(-1)
    onehot = (flat_e[:, None] == jnp.arange(n_exp, dtype=flat_e.dtype)[None, :]).astype(jnp.int32)
    csum = jnp.cumsum(onehot, axis=0)
    counts = csum[-1]
    rank = jnp.sum((csum - onehot) * onehot, axis=1)
    padded = (counts + bm - 1) // bm * bm
    pad_end = jnp.cumsum(padded)
    pad_start = pad_end - padded
    slot = pad_start[flat_e] + rank
    n_rows = _round_up(n_assign + n_exp * (bm - 1), bm)
    n_blocks = n_rows // bm
    flat_tok = jnp.repeat(jnp.arange(m, dtype=jnp.int32), TOP_K)
    assert m < 2 ** 24
    fill = jnp.zeros((n_rows, 2), F32)
    row_info = fill.at[slot].set(jnp.stack([flat_tok.astype(F32), gates.reshape(-1)], axis=1))
    row_tok = row_info[:, 0].astype(jnp.int32)
    row_gate = row_info[:, 1]
    blk_start = jnp.arange(n_blocks, dtype=jnp.int32) * bm
    blk_e = jnp.minimum(jnp.sum(pad_end[None, :] <= blk_start[:, None], axis=1), n_exp - 1).astype(jnp.int32)
    n_used = (pad_end[-1] // bm).astype(jnp.int32).reshape(1)
    yb = _experts(hn, row_tok, row_gate[:, None], blk_e, n_used, wg, wu, wd, bm)
    slot2 = slot.reshape(m, TOP_K)
    return [x[a:b] + yb[slot2[a:b, 0]] + yb[slot2[a:b, 1]] for a, b in splits]


def _pad_cols(a, width):
    return jnp.pad(a, [(0, 0)] * (a.ndim - 1) + [(0, width - a.shape[-1])])


def _rwkv_cols(a, rd, wl, al):
    o = 3 * rd
    return jnp.concatenate([a[..., :o], _pad_cols(a[..., o:o + wl], LANES),
                            _pad_cols(a[..., o + wl:o + wl + al], LANES),
                            _pad_cols(a[..., o + wl + al:], 2 * LANES)], axis=-1)


def _rwkv_cols_inv(a, rd, wl, al, gl):
    o = 3 * rd
    return jnp.concatenate([a[..., :o], a[..., o:o + wl], a[..., o + LANES:o + LANES + al],
                            a[..., o + 2 * LANES:o + 2 * LANES + gl]], axis=-1)


def _pad_rows(a, rows):
    return jnp.pad(a, ((0, rows - a.shape[0]), (0, 0)))


def _rope_tables(pos, rope):
    half = rope // 2
    assert half == LANES // 4
    inv_freq = ROPE_THETA ** (-jnp.arange(half, dtype=F32) * (2.0 / rope))
    ang = pos.astype(F32)[:, None] * inv_freq[None, :]
    cos, sin = jnp.cos(ang), jnp.sin(ang)
    z = jnp.zeros_like(cos)
    return (jnp.concatenate([cos, cos, z, z], axis=1),
            jnp.concatenate([-sin, z, z, z], axis=1),
            jnp.concatenate([z, sin, z, z], axis=1))


def kernel(x_prompt, x_sample, cache_kv, page_table, state_wkv, state_shift, g_mix, w_in, mu_shift, w0, w_decay, a0, w_iclr, w_gate_out, k_k, k_a, r_k, lnx_w, lnx_b, v0, v_res_a, v_res_b, g_qa, w_qb, g_qn, g_qr, g_kva, g_kr, w_uk, w_uv, g_kn, w_out, g_ffn, w_ffn_gate, w_ffn_up, w_ffn_down, w_router, w_exp_gate, w_exp_up, w_exp_down):
    bp, tp, d = x_prompt.shape
    bs, ts, _ = x_sample.shape
    depth = g_mix.shape[0]
    n_heads_r, head_r = r_k.shape[1], r_k.shape[2]
    rd = n_heads_r * head_r
    wl, al, gl = w_decay.shape[1], w_iclr.shape[1], w_gate_out.shape[1]
    rw_orig = 3 * rd + wl + al + gl
    ql, lat, rope, nope = g_qa.shape[1], g_kva.shape[1], g_kr.shape[1], g_qn.shape[1]
    n_heads_a, mla_v = w_uk.shape[2], w_uv.shape[3]
    assert nope == LANES and mla_v == LANES and rope == LANES // 2
    assert wl <= LANES and al <= LANES and gl <= 2 * LANES and v_res_a.shape[2] <= LANES
    past_len = page_table.shape[1] * PAGE_SIZE
    mp, ms = bp * tp, bs * ts
    scale = (nope + rope) ** -0.5
    rw = 3 * rd + 4 * LANES
    col_q, col_kv = rw, rw + ql
    chunk_p = _pick(tp, (64, 32, 16, 8))
    chunk_s = _pick(ts, (64, 32, 16, 8))

    x = jnp.concatenate([x_prompt.reshape(mp, d), x_sample.reshape(ms, d)], axis=0)
    pos = jnp.concatenate([jnp.tile(jnp.arange(tp, dtype=jnp.int32), bp),
                           jnp.tile(past_len + jnp.arange(ts, dtype=jnp.int32), bs)])
    tabs = _rope_tables(pos, rope)
    row2 = lambda a: a.reshape(1, -1)
    cache_t = jnp.transpose(cache_kv, (0, 1, 3, 2))

    shift0_p = jnp.zeros((bp, 1, rw), F32)
    wkv0_p = jnp.zeros((bp, n_heads_r, head_r, head_r), F32)
    vf_p = vf_s = None
    rows_p, rows_s, wkv_p, wkv_s, sh_p, sh_s = [], [], [], [], [], []
    for l in range(depth):
        w_in_l = w_in[l]
        w_pad = jnp.concatenate([
            _rwkv_cols(w_in_l[:, :rw_orig], rd, wl, al),
            w_in_l[:, rw_orig:rw_orig + ql + lat],
            _pad_cols(w_in_l[:, rw_orig + ql + lat:], 2 * LANES)], axis=1).astype(BF16)
        p = _in_proj(x, row2(g_mix[l]), w_pad)

        prm = {
            "mu": _rwkv_cols(row2(mu_shift[l]), rd, wl, al),
            "w0": row2(w0[l]), "a0": row2(a0[l]),
            "wdec": _pad_rows(w_decay[l], LANES).astype(BF16),
            "wiclr": _pad_rows(w_iclr[l], LANES).astype(BF16),
            "wgate": _pad_rows(w_gate_out[l], 2 * LANES).astype(BF16),
            "k_k": row2(k_k[l]), "k_a": row2(k_a[l]), "r_k": row2(r_k[l]),
            "lnx_w": row2(lnx_w[l]), "lnx_b": row2(lnx_b[l]),
        }
        if l > 0:
            prm["v0"] = row2(v0[l - 1])
            prm["vra"] = _pad_cols(v_res_a[l - 1], LANES).astype(BF16)
            prm["vrb"] = _pad_rows(v_res_b[l - 1], LANES).astype(BF16)
        oa_p, vf_p, wp = _rwkv_mixer(p, 0, bp, tp, shift0_p, wkv0_p, vf_p, prm, chunk_p)
        shift0_s = _rwkv_cols(state_shift[l], rd, wl, al)[:, None, :]
        oa_s, vf_s, ws = _rwkv_mixer(p, mp, bs, ts, shift0_s, state_wkv[l], vf_s, prm, chunk_s)

        wqb3 = w_qb[l].reshape(ql, n_heads_a, nope + rope)
        wqb_pad = jnp.concatenate([wqb3[:, :, :nope].reshape(ql, n_heads_a * nope),
                                   _pad_cols(wqb3[:, :, nope:], LANES).reshape(ql, n_heads_a * LANES)],
                                  axis=1).astype(BF16)
        mprm = {"g_qa": row2(g_qa[l]), "w_qb": wqb_pad, "g_qn": row2(g_qn[l]),
                "g_qr": _pad_cols(row2(g_qr[l]), LANES), "g_kva": row2(g_kva[l]),
                "g_kr": _pad_cols(row2(g_kr[l]), LANES)}
        q_cat, rows, c_bf, kr_bf = _mla_proj(p, col_q, col_kv, tabs, mprm, n_heads_a, rope, scale)

        wuk2 = w_uk[l].reshape(lat, n_heads_a * nope).astype(BF16)
        wuv_t = jnp.transpose(w_uv[l], (1, 2, 0)).reshape(n_heads_a * mla_v, lat).astype(BF16)
        gkn = row2(g_kn[l])
        k_cat, v_exp = _kv_expand(c_bf, kr_bf, mp, wuk2, wuv_t, gkn, n_heads_a)
        ob_p = _flash_prompt(q_cat, k_cat, v_exp, bp, tp, n_heads_a)

        wuk3 = jnp.transpose(w_uk[l], (1, 0, 2)).astype(BF16)
        wukt = jnp.transpose(w_uk[l], (1, 2, 0)).reshape(n_heads_a * nope, lat).astype(BF16)
        wuv3 = jnp.transpose(w_uv[l], (1, 0, 2)).astype(BF16)
        ob_s = _sample_attn(q_cat, c_bf, kr_bf, mp, cache_t, l, page_table, wuk3, wukt, wuv3, gkn,
                            ts, n_heads_a, rope)

        o_a = jnp.concatenate([oa_p, oa_s], axis=0)
        o_b = jnp.concatenate([ob_p, ob_s], axis=0)
        x = _out_proj(x, o_a, o_b, w_out[l].astype(BF16))

        i = l // 2
        if l % 2 == 0:
            x = _ffn_dense(x, row2(g_ffn[l]), w_ffn_gate[i].astype(BF16), w_ffn_up[i].astype(BF16),
                           w_ffn_down[i].astype(BF16))
        else:
            last = l == depth - 1
            parts = _moe(x, row2(g_ffn[l]), w_router[i], w_exp_gate[i], w_exp_up[i], w_exp_down[i],
                         [(0, mp), (mp, mp + ms)] if last else [(0, mp + ms)])
            x = None if last else parts[0]

        rows_p.append(rows[:mp].reshape(bp, tp, lat + rope))
        rows_s.append(rows[mp:].reshape(bs, ts, lat + rope))
        wkv_p.append(wp)
        wkv_s.append(ws)
        last_p = jnp.concatenate([p[b * tp + tp - 1:b * tp + tp, :rw] for b in range(bp)], axis=0)
        last_s = p[mp:].reshape(bs, ts, -1)[:, -1, :rw]
        sh_p.append(_rwkv_cols_inv(last_p, rd, wl, al, gl))
        sh_s.append(_rwkv_cols_inv(last_s, rd, wl, al, gl))

    if x is not None:
        parts = [x[:mp], x[mp:]]
    return (parts[0].reshape(bp, tp, d), parts[1].reshape(bs, ts, d),
            jnp.stack(rows_p), jnp.stack(rows_s), jnp.stack(wkv_p), jnp.stack(wkv_s),
            jnp.stack(sh_p), jnp.stack(sh_s))
```

```python
import functools
import math

import jax
import jax.numpy as jnp
from jax import lax
from jax.experimental import pallas as pl
from jax.experimental.pallas import tpu as pltpu

F32 = jnp.float32
BF16 = jnp.bfloat16
RMS_EPS = 1e-6
GN_EPS = 64e-5
ROPE_THETA = 10000.0
TOP_K = 2
PAGE_SIZE = 128
LANES = 128
VMEM_LIMIT = 56 * 1024 * 1024
NEUTRAL_SCORE = -1e30


def _pick(n, prefs):
    for p in prefs:
        if n % p == 0:
            return p
    raise ValueError(f"no tile for {n} in {prefs}")


def _round_up(n, m):
    return (n + m - 1) // m * m


def _params(*sem):
    return pltpu.CompilerParams(dimension_semantics=sem, vmem_limit_bytes=VMEM_LIMIT)


def _dot(a, b, **kw):
    return jnp.dot(a, b, preferred_element_type=F32, **kw)


def _dot_nt(a, b):
    return lax.dot_general(a, b, (((1,), (1,)), ((), ())), preferred_element_type=F32)


def _dot_tn(a, b):
    return lax.dot_general(a, b, (((0,), (0,)), ((), ())), preferred_element_type=F32)


def _rms(x, g, n=None):
    n = x.shape[-1] if n is None else n
    ms = jnp.sum(x * x, axis=-1, keepdims=True) * (1.0 / n)
    return x * lax.rsqrt(ms + RMS_EPS) * g


def _in_proj_kernel(x_ref, g_ref, w_ref, o_ref, hn_ref):
    @pl.when(pl.program_id(1) == 0)
    def _():
        hn_ref[...] = _rms(x_ref[...], g_ref[...]).astype(BF16)

    o_ref[...] = _dot(hn_ref[...], w_ref[...])


def _in_proj(x, g, w):
    m, d = x.shape
    n = w.shape[1]
    tm = _pick(m, (1024, 512, 256, 128, 64, 32, 16, 8))
    tn = _pick(n, (1536, 512, 256, 128))
    return pl.pallas_call(
        _in_proj_kernel,
        grid=(m // tm, n // tn),
        in_specs=[pl.BlockSpec((tm, d), lambda i, j: (i, 0)),
                  pl.BlockSpec((1, d), lambda i, j: (0, 0)),
                  pl.BlockSpec((d, tn), lambda i, j: (0, j))],
        out_specs=pl.BlockSpec((tm, tn), lambda i, j: (i, j)),
        out_shape=jax.ShapeDtypeStruct((m, n), F32),
        scratch_shapes=[pltpu.VMEM((tm, d), BF16)],
        compiler_params=_params("parallel", "arbitrary"),
        name="in_proj",
    )(x, g, w)


def _rwkv_kernel(*refs, n_heads, head, chunk, has_vres, head_group):
    it = iter(refs)
    p_ref, shift0_ref, s0_ref = next(it), next(it), next(it)
    vfirst_ref = next(it) if has_vres else None
    mu_ref, w0_ref, wdec_ref, a0_ref, wiclr_ref, wgate_ref = (next(it) for _ in range(6))
    kk_ref, ka_ref, rk_ref, lnw_ref, lnb_ref = (next(it) for _ in range(5))
    if has_vres:
        v0_ref, vra_ref, vrb_ref = next(it), next(it), next(it)
    o_ref = next(it)
    vout_ref = None if has_vres else next(it)
    slast_ref = next(it)
    s_scr, carry_scr, y_scr = next(it), next(it), next(it)

    c_idx = pl.program_id(1)
    rd = n_heads * head
    rw = mu_ref.shape[1]
    C = chunk

    @pl.when(c_idx == 0)
    def _():
        s_scr[...] = s0_ref[0]
        carry_scr[...] = shift0_ref[0]

    p = p_ref[:, :rw]
    rid = lax.broadcasted_iota(jnp.int32, p.shape, 0)
    prev = jnp.where(rid == 0, carry_scr[...], pltpu.roll(p, 1, 0))
    carry_scr[...] = p[C - 1:C, :]
    m = p + (prev - p) * mu_ref[...]

    r = m[:, :rd]
    k = m[:, rd:2 * rd]
    v = m[:, 2 * rd:3 * rd]
    pw = m[:, 3 * rd:3 * rd + LANES]
    pa = m[:, 3 * rd + LANES:3 * rd + 2 * LANES]
    pg = m[:, 3 * rd + 2 * LANES:]

    u = w0_ref[...] + _dot(jnp.tanh(pw).astype(BF16), wdec_ref[...])
    lw = (-math.exp(-0.5)) * jax.nn.sigmoid(u)
    a = jax.nn.sigmoid(a0_ref[...] + _dot(pa.astype(BF16), wiclr_ref[...]))
    g = _dot(jax.nn.sigmoid(pg).astype(BF16), wgate_ref[...])
    if has_vres:
        lo = _dot(v.astype(BF16), vra_ref[...])
        v = v + (vfirst_ref[...] - v) * jax.nn.sigmoid(v0_ref[...] + _dot(lo.astype(BF16), vrb_ref[...]))
    else:
        vout_ref[...] = v
    kk_raw = k * kk_ref[...]
    k = k * (1.0 + (a - 1.0) * ka_ref[...])

    row = lax.broadcasted_iota(jnp.int32, (C, C), 0)
    col = lax.broadcasted_iota(jnp.int32, (C, C), 1)
    incl = col <= row
    strict = col < row
    cum = _dot(incl.astype(F32), lw, precision=lax.Precision.HIGHEST)
    dec = jnp.exp(cum)
    dec_prev = jnp.exp(cum - lw)
    dec_inv = jnp.exp(-cum)

    eye = (col == row).astype(F32)
    mask2 = jnp.concatenate([strict, incl], axis=0)

    for h0 in range(0, n_heads, head_group):
        hs = range(h0, min(h0 + head_group, n_heads))
        sls = {h: slice(h * head, (h + 1) * head) for h in hs}
        ar, bk, vb, s_old = {}, {}, {}, {}
        for h in hs:
            sl = sls[h]
            kk = kk_raw[:, sl]
            kk = kk / jnp.maximum(jnp.sqrt(jnp.sum(kk * kk, axis=-1, keepdims=True)), 1e-12)
            a_t = (-kk) * dec_prev[:, sl]
            r_t = r[:, sl] * dec[:, sl]
            b_t = (kk * a[:, sl]) * dec_inv[:, sl]
            k_t = k[:, sl] * dec_inv[:, sl]
            ar[h] = jnp.concatenate([a_t, r_t], axis=0).astype(BF16)
            bk[h] = (b_t.astype(BF16), k_t.astype(BF16))
            vb[h] = v[:, sl].astype(BF16)
            s_old[h] = s_scr[h]
        g_b = {h: jnp.where(mask2, _dot_nt(ar[h], bk[h][0]), 0.0) for h in hs}
        g_k = {h: jnp.where(mask2, _dot_nt(ar[h], bk[h][1]), 0.0).astype(BF16) for h in hs}
        pw_l = {h: g_b[h][:C] for h in hs}
        inv = {h: eye + pw_l[h] for h in hs}
        n = 1
        while 2 * n < C:
            pw_b = {h: pw_l[h].astype(BF16) for h in hs}
            pw_l = {h: _dot(pw_b[h], pw_b[h]) for h in hs}
            inv = {h: inv[h] + _dot(inv[h].astype(BF16), pw_l[h].astype(BF16)) for h in hs}
            n *= 2
        xy = {h: _dot_nt(ar[h], s_old[h].astype(BF16)) + _dot(g_k[h], vb[h]) for h in hs}
        u_b = {h: _dot(inv[h].astype(BF16), xy[h][:C].astype(BF16)).astype(BF16) for h in hs}
        y = {h: xy[h][C:] + _dot(g_b[h][C:].astype(BF16), u_b[h]) for h in hs}
        for h in hs:
            sl = sls[h]
            upd = _dot_tn(jnp.concatenate([u_b[h], vb[h]], axis=0),
                          jnp.concatenate(bk[h], axis=0))
            s_scr[h] = (s_old[h] + upd) * dec[C - 1:C, sl]
            mean = jnp.mean(y[h], axis=-1, keepdims=True)
            yc = y[h] - mean
            var = jnp.mean(yc * yc, axis=-1, keepdims=True)
            yn = yc * lax.rsqrt(var + GN_EPS) * lnw_ref[:, sl] + lnb_ref[:, sl]
            bonus = jnp.sum(r[:, sl] * k[:, sl] * rk_ref[:, sl], axis=-1, keepdims=True) * v[:, sl]
            y_scr[:, sl] = yn + bonus

    o_ref[...] = (y_scr[...] * g).astype(o_ref.dtype)

    @pl.when(c_idx == pl.num_programs(1) - 1)
    def _():
        slast_ref[0] = s_scr[...]


def _rwkv_mixer(p, row0, bsz, t, shift0, s0, vfirst, prm, chunk):
    n_heads, head = s0.shape[1], s0.shape[2]
    rd = n_heads * head
    rw = prm["mu"].shape[1]
    has_vres = vfirst is not None
    nc = t // chunk
    blk0 = row0 // chunk
    np_ = p.shape[1]

    def full(a):
        return pl.BlockSpec(a.shape, lambda b, c: (0,) * a.ndim)

    ins = [p, shift0, s0]
    specs = [pl.BlockSpec((chunk, np_), lambda b, c: (blk0 + b * nc + c, 0)),
             pl.BlockSpec((1, 1, rw), lambda b, c: (b, 0, 0)),
             pl.BlockSpec((1, n_heads, head, head), lambda b, c: (b, 0, 0, 0))]
    if has_vres:
        ins.append(vfirst)
        specs.append(pl.BlockSpec((chunk, rd), lambda b, c: (b * nc + c, 0)))
    names = ["mu", "w0", "wdec", "a0", "wiclr", "wgate", "k_k", "k_a", "r_k", "lnx_w", "lnx_b"]
    if has_vres:
        names += ["v0", "vra", "vrb"]
    for nm in names:
        ins.append(prm[nm])
        specs.append(full(prm[nm]))

    row_spec = pl.BlockSpec((chunk, rd), lambda b, c: (b * nc + c, 0))
    outs = [jax.ShapeDtypeStruct((bsz * t, rd), BF16)]
    out_specs = [row_spec]
    if not has_vres:
        outs.append(jax.ShapeDtypeStruct((bsz * t, rd), F32))
        out_specs.append(row_spec)
    outs.append(jax.ShapeDtypeStruct(s0.shape, F32))
    out_specs.append(pl.BlockSpec((1, n_heads, head, head), lambda b, c: (b, 0, 0, 0)))

    res = pl.pallas_call(
        functools.partial(_rwkv_kernel, n_heads=n_heads, head=head, chunk=chunk, has_vres=has_vres,
                          head_group=n_heads),
        grid=(bsz, nc),
        in_specs=specs,
        out_specs=out_specs,
        out_shape=outs,
        scratch_shapes=[pltpu.VMEM((n_heads, head, head), F32),
                        pltpu.VMEM((1, rw), F32),
                        pltpu.VMEM((chunk, rd), F32)],
        compiler_params=_params("parallel", "arbitrary"),
        name="rwkv_mixer",
    )(*ins)
    if has_vres:
        o, s_last = res
        return o, vfirst, s_last
    o, v_out, s_last = res
    return o, v_out, s_last


def _rotate(x, cos, sin_a, sin_b):
    half = LANES // 4
    return x * cos + pltpu.roll(x, LANES - half, 1) * sin_a + pltpu.roll(x, half, 1) * sin_b


def _mla_proj_kernel(qa_ref, kv_ref, cos_ref, sina_ref, sinb_ref, gqa_ref, wqb_ref, gqn_ref, gqr_ref,
                     gkva_ref, gkr_ref, q_ref, rows_ref, c_ref, kr_ref, *, n_heads, rope, scale):
    cos, sin_a, sin_b = cos_ref[...], sina_ref[...], sinb_ref[...]
    hq = _rms(qa_ref[...], gqa_ref[...]).astype(BF16)
    q = _dot(hq, wqb_ref[...])
    hw = n_heads * LANES
    for h in range(n_heads):
        qn = _rms(q[:, h * LANES:(h + 1) * LANES], gqn_ref[...]) * scale
        qr = _rms(q[:, hw + h * LANES:hw + (h + 1) * LANES], gqr_ref[...], n=rope)
        qr = _rotate(qr, cos, sin_a, sin_b) * scale
        q_ref[:, 2 * h * LANES:(2 * h + 1) * LANES] = qn.astype(q_ref.dtype)
        q_ref[:, (2 * h + 1) * LANES:(2 * h + 2) * LANES] = qr.astype(q_ref.dtype)
    kv = kv_ref[...]
    lat = gkva_ref.shape[1]
    c = _rms(kv[:, :lat], gkva_ref[...])
    kr = _rotate(_rms(kv[:, lat:lat + LANES], gkr_ref[...], n=rope), cos, sin_a, sin_b)
    rows_ref[:, :lat] = c
    rows_ref[:, lat:] = kr[:, :rope]
    c_ref[...] = c.astype(c_ref.dtype)
    kr_ref[...] = kr.astype(kr_ref.dtype)


def _mla_proj(p, col_q, col_kv, tabs, prm, n_heads, rope, scale):
    m = p.shape[0]
    tm = _pick(m, (512, 256, 128, 64, 32, 16, 8))
    qw = prm["g_qa"].shape[1]
    lat = prm["g_kva"].shape[1]
    assert qw == 512 and col_q % qw == 0 and col_kv % qw == 0

    def full(a):
        return pl.BlockSpec(a.shape, lambda i: (0,) * a.ndim)

    tab_spec = pl.BlockSpec((tm, LANES), lambda i: (i, 0))
    ws = [prm["g_qa"], prm["w_qb"], prm["g_qn"], prm["g_qr"], prm["g_kva"], prm["g_kr"]]
    return pl.pallas_call(
        functools.partial(_mla_proj_kernel, n_heads=n_heads, rope=rope, scale=scale),
        grid=(m // tm,),
        in_specs=[pl.BlockSpec((tm, qw), lambda i: (i, col_q // qw)),
                  pl.BlockSpec((tm, qw), lambda i: (i, col_kv // qw)),
                  tab_spec, tab_spec, tab_spec] + [full(a) for a in ws],
        out_specs=[pl.BlockSpec((tm, 2 * n_heads * LANES), lambda i: (i, 0)),
                   pl.BlockSpec((tm, lat + rope), lambda i: (i, 0)),
                   pl.BlockSpec((tm, lat), lambda i: (i, 0)),
                   pl.BlockSpec((tm, LANES), lambda i: (i, 0))],
        out_shape=[jax.ShapeDtypeStruct((m, 2 * n_heads * LANES), BF16),
                   jax.ShapeDtypeStruct((m, lat + rope), F32),
                   jax.ShapeDtypeStruct((m, lat), BF16),
                   jax.ShapeDtypeStruct((m, LANES), BF16)],
        compiler_params=_params("parallel"),
        name="mla_proj",
    )(p, p, *tabs, *ws)


def _kv_expand_kernel(c_ref, kr_ref, wuk_ref, wuv_ref, gkn_ref, k_ref, v_ref, *, n_heads):
    c = c_ref[...]
    ke = _dot(c, wuk_ref[...])
    kr = kr_ref[...]
    for h in range(n_heads):
        kh = _rms(ke[:, h * LANES:(h + 1) * LANES], gkn_ref[...])
        k_ref[:, 2 * h * LANES:(2 * h + 1) * LANES] = kh.astype(k_ref.dtype)
        k_ref[:, (2 * h + 1) * LANES:(2 * h + 2) * LANES] = kr
    v_ref[...] = _dot_nt(wuv_ref[...], c).astype(v_ref.dtype)


def _kv_expand(c, kr, mp, wuk, wuv_t, gkn, n_heads):
    lat = c.shape[1]
    tm = _pick(mp, (512, 256, 128))
    assert wuk.shape[1] == n_heads * LANES and wuv_t.shape[0] == n_heads * LANES

    def full(a):
        return pl.BlockSpec(a.shape, lambda i: (0,) * a.ndim)

    return pl.pallas_call(
        functools.partial(_kv_expand_kernel, n_heads=n_heads),
        grid=(mp // tm,),
        in_specs=[pl.BlockSpec((tm, lat), lambda i: (i, 0)),
                  pl.BlockSpec((tm, LANES), lambda i: (i, 0)),
                  full(wuk), full(wuv_t), full(gkn)],
        out_specs=[pl.BlockSpec((tm, 2 * n_heads * LANES), lambda i: (i, 0)),
                   pl.BlockSpec((n_heads * LANES, tm), lambda i: (0, i))],
        out_shape=[jax.ShapeDtypeStruct((mp, 2 * n_heads * LANES), BF16),
                   jax.ShapeDtypeStruct((n_heads * LANES, mp), BF16)],
        compiler_params=_params("parallel"),
        name="kv_expand",
    )(c, kr, wuk, wuv_t, gkn)


def _flash_kernel(qi_ref, ki_ref, q_ref, k_ref, v_ref, o_ref, m_scr, l_scr, acc_scr, *, tq, tk):
    pair = pl.program_id(2)
    qi, ki = qi_ref[pair], ki_ref[pair]

    @pl.when(ki == 0)
    def _():
        m_scr[...] = jnp.full(m_scr.shape, -jnp.inf, F32)
        l_scr[...] = jnp.zeros(l_scr.shape, F32)
        acc_scr[...] = jnp.zeros(acc_scr.shape, F32)

    s = _dot_nt(k_ref[...], q_ref[...])
    key = ki * tk + lax.broadcasted_iota(jnp.int32, s.shape, 0)
    qry = qi * tq + lax.broadcasted_iota(jnp.int32, s.shape, 1)
    s = jnp.where(key <= qry, s, -jnp.inf)
    m_old = m_scr[...]
    m_new = jnp.maximum(m_old, jnp.max(s, axis=0, keepdims=True))
    alpha = jnp.exp(m_old - m_new)
    p = jnp.exp(s - m_new)
    l_scr[...] = alpha * l_scr[...] + jnp.sum(p, axis=0, keepdims=True)
    acc_scr[...] = alpha * acc_scr[...] + _dot(v_ref[...], p.astype(BF16))
    m_scr[...] = m_new

    @pl.when((ki + 1) * tk >= (qi + 1) * tq)
    def _():
        o_ref[...] = (acc_scr[...] / l_scr[...]).T.astype(o_ref.dtype)


def _flash_prompt(q, k, v, bsz, t, n_heads):
    tk = _pick(t, (1024, 512, 256, 128))
    tq = tk if tk >= 1024 else (2 * tk if t % (2 * tk) == 0 else tk)
    nq, nk = t // tq, t // tk
    pairs = [(qi, ki) for qi in range(nq) for ki in range((qi + 1) * tq // tk)]
    qi_arr = jnp.asarray([pr[0] for pr in pairs], jnp.int32)
    ki_arr = jnp.asarray([pr[1] for pr in pairs], jnp.int32)
    grid_spec = pltpu.PrefetchScalarGridSpec(
        num_scalar_prefetch=2,
        grid=(bsz, n_heads, len(pairs)),
        in_specs=[pl.BlockSpec((tq, 2 * LANES), lambda b, h, pr, qa, ka: (b * nq + qa[pr], h)),
                  pl.BlockSpec((tk, 2 * LANES), lambda b, h, pr, qa, ka: (b * nk + ka[pr], h)),
                  pl.BlockSpec((LANES, tk), lambda b, h, pr, qa, ka: (h, b * nk + ka[pr]))],
        out_specs=pl.BlockSpec((tq, LANES), lambda b, h, pr, qa, ka: (b * nq + qa[pr], h)),
        scratch_shapes=[pltpu.VMEM((1, tq), F32), pltpu.VMEM((1, tq), F32), pltpu.VMEM((LANES, tq), F32)],
    )
    return pl.pallas_call(
        functools.partial(_flash_kernel, tq=tq, tk=tk),
        grid_spec=grid_spec,
        out_shape=jax.ShapeDtypeStruct((bsz * t, n_heads * LANES), BF16),
        compiler_params=_params("parallel", "parallel", "arbitrary"),
        name="flash_prompt",
    )(qi_arr, ki_arr, q, k, v)


def _sample_attn_kernel(pt_ref, q_ref, cn_ref, krn_ref, wuk3_ref, wukt_ref, wuv3_ref, gkn_ref, *rest,
                        n_heads, n_tok, n_page_in, sub_pages, lat, rope):
    page_refs = rest[:n_page_in]
    o_ref = rest[n_page_in]
    lhs_scr, qr_scr, m_scr, l_scr, acc_scr, s_scr, c_scr = rest[n_page_in + 1:]
    j = pl.program_id(1)
    nq = n_heads * n_tok
    hk = n_heads * LANES

    @pl.when(j == 0)
    def _():
        lhs_scr[:hk, :] = wukt_ref[...]
        for h in range(n_heads):
            qn = (q_ref[:, 2 * h * LANES:(2 * h + 1) * LANES].astype(F32) * gkn_ref[...]).astype(BF16)
            lhs_scr[hk + h * n_tok:hk + (h + 1) * n_tok, :] = _dot_nt(qn, wuk3_ref[h]).astype(BF16)
            qr_scr[h * n_tok:(h + 1) * n_tok, :] = q_ref[:, (2 * h + 1) * LANES:(2 * h + 2) * LANES]
        m_scr[...] = jnp.full(m_scr.shape, -jnp.inf, F32)
        l_scr[...] = jnp.zeros(l_scr.shape, F32)
        acc_scr[...] = jnp.zeros(acc_scr.shape, F32)
        s_scr[...] = jnp.full(s_scr.shape, NEUTRAL_SCORE, F32)
        c_scr[...] = jnp.zeros(c_scr.shape, BF16)

    def scores(big, s_rope):
        npos = big.shape[1]
        invs = []
        for h in range(n_heads):
            kh = big[h * LANES:(h + 1) * LANES]
            ms = jnp.sum(kh * kh, axis=0, keepdims=True) * (1.0 / LANES)
            invs.append(jnp.broadcast_to(lax.rsqrt(ms + RMS_EPS), (n_tok, npos)))
        return big[hk:] * jnp.concatenate(invs, axis=0) + s_rope

    diag = (lax.broadcasted_iota(jnp.int32, (nq, nq), 0) == lax.broadcasted_iota(jnp.int32, (nq, nq), 1))

    def as_row(col):
        return jnp.sum(jnp.where(diag, col, 0.0), axis=0, keepdims=True)

    def load_tile(sub):
        pages = [page_refs[sub * sub_pages + i][0, 0] for i in range(sub_pages)]
        c_t = jnp.concatenate([pg[:lat].astype(BF16) for pg in pages], axis=1)
        kr_t = jnp.concatenate([pg[lat:lat + rope].astype(BF16) for pg in pages], axis=1)
        return c_t, _dot(lhs_scr[...], c_t), _dot(qr_scr[:, :rope], kr_t)

    def softmax_update():
        s = s_scr[...]
        m_old = m_scr[...]
        m_new = jnp.maximum(m_old, jnp.max(s, axis=-1, keepdims=True))
        alpha = jnp.exp(m_old - m_new)
        p = jnp.exp(s - m_new)
        l_scr[...] = alpha * l_scr[...] + jnp.sum(p, axis=-1, keepdims=True)
        acc_scr[...] = as_row(alpha) * acc_scr[...] + _dot_nt(c_scr[...], p.astype(BF16))
        m_scr[...] = m_new

    n_sub = n_page_in // sub_pages
    nxt = load_tile(0)
    softmax_update()
    for sub in range(n_sub):
        c_t, big, s_rope = nxt
        if sub + 1 < n_sub:
            nxt = load_tile(sub + 1)
        npos = c_t.shape[1]
        s_scr[:, sub * npos:(sub + 1) * npos] = scores(big, s_rope)
        c_scr[:, sub * npos:(sub + 1) * npos] = c_t

    @pl.when(j == pl.num_programs(1) - 1)
    def _():
        softmax_update()
        c_n = cn_ref[...]
        row = lax.broadcasted_iota(jnp.int32, (nq, n_tok), 0)
        col = lax.broadcasted_iota(jnp.int32, (nq, n_tok), 1)
        s = scores(_dot_nt(lhs_scr[...], c_n), _dot_nt(qr_scr[:, :rope], krn_ref[:, :rope]))
        s = jnp.where(col <= (row % n_tok), s, -jnp.inf)
        m_old = m_scr[...]
        m_new = jnp.maximum(m_old, jnp.max(s, axis=-1, keepdims=True))
        alpha = jnp.exp(m_old - m_new)
        p = jnp.exp(s - m_new)
        l_fin = alpha * l_scr[...] + jnp.sum(p, axis=-1, keepdims=True)
        acc_fin = (as_row(alpha) * acc_scr[...]).T + _dot(p.astype(BF16), c_n)
        o_lat = (acc_fin / l_fin).astype(BF16)
        for h in range(n_heads):
            o_h = _dot(o_lat[h * n_tok:(h + 1) * n_tok], wuv3_ref[h])
            o_ref[:, h * LANES:(h + 1) * LANES] = o_h.astype(o_ref.dtype)


def _sample_attn(q, c_new, kr_new, row0, cache_t, layer, page_table, wuk3, wukt, wuv3, gkn,
                 n_tok, n_heads, rope):
    bs, n_pages = page_table.shape
    lat = c_new.shape[1]
    kvw, page = cache_t.shape[2], cache_t.shape[3]
    n_page_in = _pick(n_pages, (16, 8, 4, 2, 1))
    sub_pages = _pick(n_page_in, (4, 2, 1))
    steps = n_pages // n_page_in
    blk0 = row0 // n_tok
    nq = n_heads * n_tok

    def full(a):
        return pl.BlockSpec(a.shape, lambda b, j, pt: (0,) * a.ndim)

    def page_spec(i):
        return pl.BlockSpec((1, 1, kvw, page),
                            lambda b, j, pt: (pt[b * n_pages + j * n_page_in + i], layer, 0, 0))

    grid_spec = pltpu.PrefetchScalarGridSpec(
        num_scalar_prefetch=1,
        grid=(bs, steps),
        in_specs=[pl.BlockSpec((n_tok, 2 * n_heads * LANES), lambda b, j, pt: (blk0 + b, 0)),
                  pl.BlockSpec((n_tok, lat), lambda b, j, pt: (blk0 + b, 0)),
                  pl.BlockSpec((n_tok, LANES), lambda b, j, pt: (blk0 + b, 0)),
                  full(wuk3), full(wukt), full(wuv3), full(gkn)] + [page_spec(i) for i in range(n_page_in)],
        out_specs=pl.BlockSpec((n_tok, n_heads * LANES), lambda b, j, pt: (b, 0)),
        scratch_shapes=[pltpu.VMEM((n_heads * LANES + nq, lat), BF16), pltpu.VMEM((nq, LANES), BF16),
                        pltpu.VMEM((nq, 1), F32), pltpu.VMEM((nq, 1), F32), pltpu.VMEM((lat, nq), F32),
                        pltpu.VMEM((nq, n_page_in * page), F32), pltpu.VMEM((lat, n_page_in * page), BF16)],
    )
    return pl.pallas_call(
        functools.partial(_sample_attn_kernel, n_heads=n_heads, n_tok=n_tok, n_page_in=n_page_in,
                          sub_pages=sub_pages, lat=lat, rope=rope),
        grid_spec=grid_spec,
        out_shape=jax.ShapeDtypeStruct((bs * n_tok, n_heads * LANES), BF16),
        compiler_params=_params("parallel", "arbitrary"),
        name="sample_attn",
    )(page_table.reshape(-1), q, c_new, kr_new, wuk3, wukt, wuv3, gkn, *([cache_t] * n_page_in))


def _out_proj_kernel(x_ref, a_ref, b_ref, w_ref, o_ref):
    ka = a_ref.shape[1]
    o_ref[...] = x_ref[...] + _dot(a_ref[...], w_ref[:ka, :]) + _dot(b_ref[...], w_ref[ka:, :])


def _out_proj(x, a, b, w):
    m, d = x.shape
    tm = _pick(m, (1024, 512, 256, 128, 64, 32, 16, 8))
    tn = _pick(d, (1024, 512, 256, 128))
    return pl.pallas_call(
        _out_proj_kernel,
        grid=(m // tm, d // tn),
        in_specs=[pl.BlockSpec((tm, tn), lambda i, j: (i, j)),
                  pl.BlockSpec((tm, a.shape[1]), lambda i, j: (i, 0)),
                  pl.BlockSpec((tm, b.shape[1]), lambda i, j: (i, 0)),
                  pl.BlockSpec((w.shape[0], tn), lambda i, j: (0, j))],
        out_specs=pl.BlockSpec((tm, tn), lambda i, j: (i, j)),
        out_shape=jax.ShapeDtypeStruct((m, d), F32),
        compiler_params=_params("parallel", "parallel"),
        name="out_proj",
    )(x, a, b, w)


def _ffn_kernel(x_ref, g_ref, wg_ref, wu_ref, wd_ref, o_ref, hn_ref):
    @pl.when(pl.program_id(1) == 0)
    def _():
        x = x_ref[...]
        hn_ref[...] = _rms(x, g_ref[...]).astype(BF16)
        o_ref[...] = x

    h = hn_ref[...]
    gt = _dot(h, wg_ref[...])
    act = (gt * jax.nn.sigmoid(gt) * _dot(h, wu_ref[...])).astype(BF16)
    o_ref[...] += _dot(act, wd_ref[...])


def _ffn_dense(x, g, wg, wu, wd):
    m, d = x.shape
    f = wg.shape[1]
    tm = _pick(m, (1024, 512, 256, 128, 64, 32, 16, 8))
    tf = _pick(f, (512, 256, 128))
    return pl.pallas_call(
        _ffn_kernel,
        grid=(m // tm, f // tf),
        in_specs=[pl.BlockSpec((tm, d), lambda i, j: (i, 0)),
                  pl.BlockSpec((1, d), lambda i, j: (0, 0)),
                  pl.BlockSpec((d, tf), lambda i, j: (0, j)),
                  pl.BlockSpec((d, tf), lambda i, j: (0, j)),
                  pl.BlockSpec((tf, d), lambda i, j: (j, 0))],
        out_specs=pl.BlockSpec((tm, d), lambda i, j: (i, 0)),
        out_shape=jax.ShapeDtypeStruct((m, d), F32),
        scratch_shapes=[pltpu.VMEM((tm, d), BF16)],
        compiler_params=_params("parallel", "arbitrary"),
        name="ffn_dense",
    )(x, g, wg, wu, wd)


def _router_kernel(x_ref, g_ref, wr_ref, hn_ref, lg_ref):
    hn = _rms(x_ref[...], g_ref[...])
    hn_ref[...] = hn.astype(hn_ref.dtype)
    lg_ref[...] = _dot(hn, wr_ref[...], precision=lax.Precision.HIGHEST)


def _router(x, g, wr):
    m, d = x.shape
    tm = _pick(m, (512, 256, 128, 64, 32, 16, 8))
    return pl.pallas_call(
        _router_kernel,
        grid=(m // tm,),
        in_specs=[pl.BlockSpec((tm, d), lambda i: (i, 0)),
                  pl.BlockSpec((1, d), lambda i: (0, 0)),
                  pl.BlockSpec(wr.shape, lambda i: (0, 0))],
        out_specs=[pl.BlockSpec((tm, d), lambda i: (i, 0)),
                   pl.BlockSpec((tm, wr.shape[1]), lambda i: (i, 0))],
        out_shape=[jax.ShapeDtypeStruct((m, d), F32),
                   jax.ShapeDtypeStruct((m, wr.shape[1]), F32)],
        compiler_params=_params("parallel"),
        name="router",
    )(x, g, wr)


def _experts_kernel(blk_e_ref, n_used_ref, row_tok_ref, hn_hbm, gate_ref, wg_ref, wu_ref, wd_ref, o_ref,
                    rows_buf, xs_scr, sems, *, bm):
    i, j = pl.program_id(0), pl.program_id(1)
    n_used = n_used_ref[0]

    def block_copy(slot):
        return pltpu.make_async_copy(hn_hbm.at[pl.ds(0, bm)], rows_buf.at[slot], sems.at[slot])

    def start_gather(block, slot):
        def body(r, carry):
            tok = row_tok_ref[block * bm + r]
            pltpu.make_async_copy(hn_hbm.at[pl.ds(tok, 1)], rows_buf.at[slot, pl.ds(r, 1)],
                                  sems.at[slot]).start()
            return carry
        lax.fori_loop(0, bm, body, 0, unroll=8)

    @pl.when(j == 0)
    def _():
        o_ref[...] = jnp.zeros(o_ref.shape, o_ref.dtype)

        @pl.when(i == 0)
        def _():
            start_gather(0, 0)

        @pl.when(i < n_used)
        def _():
            slot = i % 2
            block_copy(slot).wait()
            xs_scr[...] = rows_buf[slot].astype(BF16)

        @pl.when(i + 1 < n_used)
        def _():
            start_gather(i + 1, (i + 1) % 2)

    @pl.when(i < n_used)
    def _():
        x = xs_scr[...]
        gt = _dot(x, wg_ref[0].astype(BF16))
        act = (gt * jax.nn.sigmoid(gt) * _dot(x, wu_ref[0].astype(BF16))).astype(BF16)
        o_ref[...] += _dot(act, wd_ref[0].astype(BF16))

    @pl.when(j == pl.num_programs(1) - 1)
    def _():
        o_ref[...] = o_ref[...] * gate_ref[...]


def _experts(hn, row_tok, row_gate, blk_e, n_used, wg, wu, wd, bm):
    n_rows = row_tok.shape[0]
    d = hn.shape[1]
    assert hn.shape[0] >= bm
    f = wg.shape[2]
    tf = _pick(f, (256, 128))
    grid_spec = pltpu.PrefetchScalarGridSpec(
        num_scalar_prefetch=3,
        grid=(n_rows // bm, f // tf),
        in_specs=[pl.BlockSpec(memory_space=pl.ANY),
                  pl.BlockSpec((bm, 1), lambda i, j, be, nu, rt: (i, 0)),
                  pl.BlockSpec((1, d, tf), lambda i, j, be, nu, rt: (be[i], 0, j)),
                  pl.BlockSpec((1, d, tf), lambda i, j, be, nu, rt: (be[i], 0, j)),
                  pl.BlockSpec((1, tf, d), lambda i, j, be, nu, rt: (be[i], j, 0))],
        out_specs=pl.BlockSpec((bm, d), lambda i, j, be, nu, rt: (i, 0)),
        scratch_shapes=[pltpu.VMEM((2, bm, d), F32), pltpu.VMEM((bm, d), BF16),
                        pltpu.SemaphoreType.DMA((2,))],
    )
    return pl.pallas_call(
        functools.partial(_experts_kernel, bm=bm),
        grid_spec=grid_spec,
        out_shape=jax.ShapeDtypeStruct((n_rows, d), F32),
        compiler_params=_params("arbitrary", "arbitrary"),
        name="experts",
    )(blk_e, n_used, row_tok, hn, row_gate, wg, wu, wd)


def _moe(x, g, wr, wg, wu, wd, splits):
    m, d = x.shape
    n_exp = wg.shape[0]
    wr_pad = jnp.pad(wr, ((0, 0), (0, LANES - n_exp)))
    hn, logits = _router(x, g, wr_pad)
    top_logit, top_e = lax.top_k(logits[:, :n_exp], TOP_K)
    gates = jax.nn.softmax(top_logit, axis=-1)
    n_assign = m * TOP_K
    bm = _pick(m, (1024, 512, 256, 128, 64, 32, 16, 8))
    flat_e = top_e.reshape(-1)
    onehot = (flat_e[:, None] == jnp.arange(n_exp, dtype=flat_e.dtype)[None, :]).astype(jnp.int32)
    csum = jnp.cumsum(onehot, axis=0)
    counts = csum[-1]
    rank = jnp.sum((csum - onehot) * onehot, axis=1)
    padded = (counts + bm - 1) // bm * bm
    pad_end = jnp.cumsum(padded)
    pad_start = pad_end - padded
    slot = pad_start[flat_e] + rank
    n_rows = _round_up(n_assign + n_exp * (bm - 1), bm)
    n_blocks = n_rows // bm
    flat_tok = jnp.repeat(jnp.arange(m, dtype=jnp.int32), TOP_K)
    assert m < 2 ** 24
    fill = jnp.zeros((n_rows, 2), F32)
    row_info = fill.at[slot].set(jnp.stack([flat_tok.astype(F32), gates.reshape(-1)], axis=1))
    row_tok = row_info[:, 0].astype(jnp.int32)
    row_gate = row_info[:, 1]
    blk_start = jnp.arange(n_blocks, dtype=jnp.int32) * bm
    blk_e = jnp.minimum(jnp.sum(pad_end[None, :] <= blk_start[:, None], axis=1), n_exp - 1).astype(jnp.int32)
    n_used = (pad_end[-1] // bm).astype(jnp.int32).reshape(1)
    yb = _experts(hn, row_tok, row_gate[:, None], blk_e, n_used, wg, wu, wd, bm)
    slot2 = slot.reshape(m, TOP_K)
    return [x[a:b] + yb[slot2[a:b, 0]] + yb[slot2[a:b, 1]] for a, b in splits]


def _pad_cols(a, width):
    return jnp.pad(a, [(0, 0)] * (a.ndim - 1) + [(0, width - a.shape[-1])])


def _rwkv_cols(a, rd, wl, al):
    o = 3 * rd
    return jnp.concatenate([a[..., :o], _pad_cols(a[..., o:o + wl], LANES),
                            _pad_cols(a[..., o + wl:o + wl + al], LANES),
                            _pad_cols(a[..., o + wl + al:], 2 * LANES)], axis=-1)


def _rwkv_cols_inv(a, rd, wl, al, gl):
    o = 3 * rd
    return jnp.concatenate([a[..., :o], a[..., o:o + wl], a[..., o + LANES:o + LANES + al],
                            a[..., o + 2 * LANES:o + 2 * LANES + gl]], axis=-1)


def _pad_rows(a, rows):
    return jnp.pad(a, ((0, rows - a.shape[0]), (0, 0)))


def _rope_tables(pos, rope):
    half = rope // 2
    assert half == LANES // 4
    inv_freq = ROPE_THETA ** (-jnp.arange(half, dtype=F32) * (2.0 / rope))
    ang = pos.astype(F32)[:, None] * inv_freq[None, :]
    cos, sin = jnp.cos(ang), jnp.sin(ang)
    z = jnp.zeros_like(cos)
    return (jnp.concatenate([cos, cos, z, z], axis=1),
            jnp.concatenate([-sin, z, z, z], axis=1),
            jnp.concatenate([z, sin, z, z], axis=1))


def kernel(x_prompt, x_sample, cache_kv, page_table, state_wkv, state_shift, g_mix, w_in, mu_shift, w0, w_decay, a0, w_iclr, w_gate_out, k_k, k_a, r_k, lnx_w, lnx_b, v0, v_res_a, v_res_b, g_qa, w_qb, g_qn, g_qr, g_kva, g_kr, w_uk, w_uv, g_kn, w_out, g_ffn, w_ffn_gate, w_ffn_up, w_ffn_down, w_router, w_exp_gate, w_exp_up, w_exp_down):
    bp, tp, d = x_prompt.shape
    bs, ts, _ = x_sample.shape
    depth = g_mix.shape[0]
    n_heads_r, head_r = r_k.shape[1], r_k.shape[2]
    rd = n_heads_r * head_r
    wl, al, gl = w_decay.shape[1], w_iclr.shape[1], w_gate_out.shape[1]
    rw_orig = 3 * rd + wl + al + gl
    ql, lat, rope, nope = g_qa.shape[1], g_kva.shape[1], g_kr.shape[1], g_qn.shape[1]
    n_heads_a, mla_v = w_uk.shape[2], w_uv.shape[3]
    assert nope == LANES and mla_v == LANES and rope == LANES // 2
    assert wl <= LANES and al <= LANES and gl <= 2 * LANES and v_res_a.shape[2] <= LANES
    past_len = page_table.shape[1] * PAGE_SIZE
    mp, ms = bp * tp, bs * ts
    scale = (nope + rope) ** -0.5
    rw = 3 * rd + 4 * LANES
    col_q, col_kv = rw, rw + ql
    chunk_p = _pick(tp, (64, 32, 16, 8))
    chunk_s = _pick(ts, (64, 32, 16, 8))

    x = jnp.concatenate([x_prompt.reshape(mp, d), x_sample.reshape(ms, d)], axis=0)
    pos = jnp.concatenate([jnp.tile(jnp.arange(tp, dtype=jnp.int32), bp),
                           jnp.tile(past_len + jnp.arange(ts, dtype=jnp.int32), bs)])
    tabs = _rope_tables(pos, rope)
    row2 = lambda a: a.reshape(1, -1)
    cache_t = jnp.transpose(cache_kv, (0, 1, 3, 2))

    shift0_p = jnp.zeros((bp, 1, rw), F32)
    wkv0_p = jnp.zeros((bp, n_heads_r, head_r, head_r), F32)
    vf_p = vf_s = None
    rows_p, rows_s, wkv_p, wkv_s, sh_p, sh_s = [], [], [], [], [], []
    for l in range(depth):
        w_in_l = w_in[l]
        w_pad = jnp.concatenate([
            _rwkv_cols(w_in_l[:, :rw_orig], rd, wl, al),
            w_in_l[:, rw_orig:rw_orig + ql + lat],
            _pad_cols(w_in_l[:, rw_orig + ql + lat:], 2 * LANES)], axis=1).astype(BF16)
        p = _in_proj(x, row2(g_mix[l]), w_pad)

        prm = {
            "mu": _rwkv_cols(row2(mu_shift[l]), rd, wl, al),
            "w0": row2(w0[l]), "a0": row2(a0[l]),
            "wdec": _pad_rows(w_decay[l], LANES).astype(BF16),
            "wiclr": _pad_rows(w_iclr[l], LANES).astype(BF16),
            "wgate": _pad_rows(w_gate_out[l], 2 * LANES).astype(BF16),
            "k_k": row2(k_k[l]), "k_a": row2(k_a[l]), "r_k": row2(r_k[l]),
            "lnx_w": row2(lnx_w[l]), "lnx_b": row2(lnx_b[l]),
        }
        if l > 0:
            prm["v0"] = row2(v0[l - 1])
            prm["vra"] = _pad_cols(v_res_a[l - 1], LANES).astype(BF16)
            prm["vrb"] = _pad_rows(v_res_b[l - 1], LANES).astype(BF16)
        oa_p, vf_p, wp = _rwkv_mixer(p, 0, bp, tp, shift0_p, wkv0_p, vf_p, prm, chunk_p)
        shift0_s = _rwkv_cols(state_shift[l], rd, wl, al)[:, None, :]
        oa_s, vf_s, ws = _rwkv_mixer(p, mp, bs, ts, shift0_s, state_wkv[l], vf_s, prm, chunk_s)

        wqb3 = w_qb[l].reshape(ql, n_heads_a, nope + rope)
        wqb_pad = jnp.concatenate([wqb3[:, :, :nope].reshape(ql, n_heads_a * nope),
                                   _pad_cols(wqb3[:, :, nope:], LANES).reshape(ql, n_heads_a * LANES)],
                                  axis=1).astype(BF16)
        mprm = {"g_qa": row2(g_qa[l]), "w_qb": wqb_pad, "g_qn": row2(g_qn[l]),
                "g_qr": _pad_cols(row2(g_qr[l]), LANES), "g_kva": row2(g_kva[l]),
                "g_kr": _pad_cols(row2(g_kr[l]), LANES)}
        q_cat, rows, c_bf, kr_bf = _mla_proj(p, col_q, col_kv, tabs, mprm, n_heads_a, rope, scale)

        wuk2 = w_uk[l].reshape(lat, n_heads_a * nope).astype(BF16)
        wuv_t = jnp.transpose(w_uv[l], (1, 2, 0)).reshape(n_heads_a * mla_v, lat).astype(BF16)
        gkn = row2(g_kn[l])
        k_cat, v_exp = _kv_expand(c_bf, kr_bf, mp, wuk2, wuv_t, gkn, n_heads_a)
        ob_p = _flash_prompt(q_cat, k_cat, v_exp, bp, tp, n_heads_a)

        wuk3 = jnp.transpose(w_uk[l], (1, 0, 2)).astype(BF16)
        wukt = jnp.transpose(w_uk[l], (1, 2, 0)).reshape(n_heads_a * nope, lat).astype(BF16)
        wuv3 = jnp.transpose(w_uv[l], (1, 0, 2)).astype(BF16)
        ob_s = _sample_attn(q_cat, c_bf, kr_bf, mp, cache_t, l, page_table, wuk3, wukt, wuv3, gkn,
                            ts, n_heads_a, rope)

        o_a = jnp.concatenate([oa_p, oa_s], axis=0)
        o_b = jnp.concatenate([ob_p, ob_s], axis=0)
        x = _out_proj(x, o_a, o_b, w_out[l].astype(BF16))

        i = l // 2
        if l % 2 == 0:
            x = _ffn_dense(x, row2(g_ffn[l]), w_ffn_gate[i].astype(BF16), w_ffn_up[i].astype(BF16),
                           w_ffn_down[i].astype(BF16))
        else:
            last = l == depth - 1
            parts = _moe(x, row2(g_ffn[l]), w_router[i], w_exp_gate[i], w_exp_up[i], w_exp_down[i],
                         [(0, mp), (mp, mp + ms)] if last else [(0, mp + ms)])
            x = None if last else parts[0]

        rows_p.append(rows[:mp].reshape(bp, tp, lat + rope))
        rows_s.append(rows[mp:].reshape(bs, ts, lat + rope))
        wkv_p.append(wp)
        wkv_s.append(ws)
        last_p = jnp.concatenate([p[b * tp + tp - 1:b * tp + tp, :rw] for b in range(bp)], axis=0)
        last_s = p[mp:].reshape(bs, ts, -1)[:, -1, :rw]
        sh_p.append(_rwkv_cols_inv(last_p, rd, wl, al, gl))
        sh_s.append(_rwkv_cols_inv(last_s, rd, wl, al, gl))

    if x is not None:
        parts = [x[:mp], x[mp:]]
    return (parts[0].reshape(bp, tp, d), parts[1].reshape(bs, ts, d),
            jnp.stack(rows_p), jnp.stack(rows_s), jnp.stack(wkv_p), jnp.stack(wkv_s),
            jnp.stack(sh_p), jnp.stack(sh_s))
```

```python
import functools
import math

import jax
import jax.numpy as jnp
from jax import lax
from jax.experimental import pallas as pl
from jax.experimental.pallas import tpu as pltpu

F32 = jnp.float32
BF16 = jnp.bfloat16
RMS_EPS = 1e-6
GN_EPS = 64e-5
ROPE_THETA = 10000.0
TOP_K = 2
PAGE_SIZE = 128
LANES = 128
VMEM_LIMIT = 56 * 1024 * 1024
NEUTRAL_SCORE = -1e30


def _pick(n, prefs):
    for p in prefs:
        if n % p == 0:
            return p
    raise ValueError(f"no tile for {n} in {prefs}")


def _round_up(n, m):
    return (n + m - 1) // m * m


def _params(*sem):
    return pltpu.CompilerParams(dimension_semantics=sem, vmem_limit_bytes=VMEM_LIMIT)


def _dot(a, b, **kw):
    return jnp.dot(a, b, preferred_element_type=F32, **kw)


def _dot_nt(a, b):
    return lax.dot_general(a, b, (((1,), (1,)), ((), ())), preferred_element_type=F32)


def _dot_tn(a, b):
    return lax.dot_general(a, b, (((0,), (0,)), ((), ())), preferred_element_type=F32)


def _rms(x, g, n=None):
    n = x.shape[-1] if n is None else n
    ms = jnp.sum(x * x, axis=-1, keepdims=True) * (1.0 / n)
    return x * lax.rsqrt(ms + RMS_EPS) * g


def _in_proj_kernel(x_ref, g_ref, w_ref, o_ref, hn_ref):
    @pl.when(pl.program_id(1) == 0)
    def _():
        hn_ref[...] = _rms(x_ref[...], g_ref[...]).astype(BF16)

    o_ref[...] = _dot(hn_ref[...], w_ref[...])


def _in_proj(x, g, w):
    m, d = x.shape
    n = w.shape[1]
    tm = _pick(m, (1024, 512, 256, 128, 64, 32, 16, 8))
    tn = _pick(n, (1536, 512, 256, 128))
    return pl.pallas_call(
        _in_proj_kernel,
        grid=(m // tm, n // tn),
        in_specs=[pl.BlockSpec((tm, d), lambda i, j: (i, 0)),
                  pl.BlockSpec((1, d), lambda i, j: (0, 0)),
                  pl.BlockSpec((d, tn), lambda i, j: (0, j))],
        out_specs=pl.BlockSpec((tm, tn), lambda i, j: (i, j)),
        out_shape=jax.ShapeDtypeStruct((m, n), F32),
        scratch_shapes=[pltpu.VMEM((tm, d), BF16)],
        compiler_params=_params("parallel", "arbitrary"),
        name="in_proj",
    )(x, g, w)


def _rwkv_kernel(*refs, n_heads, head, chunk, has_vres, head_group):
    it = iter(refs)
    p_ref, shift0_ref, s0_ref = next(it), next(it), next(it)
    vfirst_ref = next(it) if has_vres else None
    mu_ref, w0_ref, wdec_ref, a0_ref, wiclr_ref, wgate_ref = (next(it) for _ in range(6))
    kk_ref, ka_ref, rk_ref, lnw_ref, lnb_ref = (next(it) for _ in range(5))
    if has_vres:
        v0_ref, vra_ref, vrb_ref = next(it), next(it), next(it)
    o_ref = next(it)
    vout_ref = None if has_vres else next(it)
    slast_ref = next(it)
    s_scr, carry_scr, y_scr = next(it), next(it), next(it)

    c_idx = pl.program_id(1)
    rd = n_heads * head
    rw = mu_ref.shape[1]
    C = chunk

    @pl.when(c_idx == 0)
    def _():
        s_scr[...] = s0_ref[0]
        carry_scr[...] = shift0_ref[0]

    p = p_ref[:, :rw]
    rid = lax.broadcasted_iota(jnp.int32, p.shape, 0)
    prev = jnp.where(rid == 0, carry_scr[...], pltpu.roll(p, 1, 0))
    carry_scr[...] = p[C - 1:C, :]
    m = p + (prev - p) * mu_ref[...]

    r = m[:, :rd]
    k = m[:, rd:2 * rd]
    v = m[:, 2 * rd:3 * rd]
    pw = m[:, 3 * rd:3 * rd + LANES]
    pa = m[:, 3 * rd + LANES:3 * rd + 2 * LANES]
    pg = m[:, 3 * rd + 2 * LANES:]

    u = w0_ref[...] + _dot(jnp.tanh(pw).astype(BF16), wdec_ref[...])
    lw = (-math.exp(-0.5)) * jax.nn.sigmoid(u)
    a = jax.nn.sigmoid(a0_ref[...] + _dot(pa.astype(BF16), wiclr_ref[...]))
    g = _dot(jax.nn.sigmoid(pg).astype(BF16), wgate_ref[...])
    if has_vres:
        lo = _dot(v.astype(BF16), vra_ref[...])
        v = v + (vfirst_ref[...] - v) * jax.nn.sigmoid(v0_ref[...] + _dot(lo.astype(BF16), vrb_ref[...]))
    else:
        vout_ref[...] = v
    kk_raw = k * kk_ref[...]
    k = k * (1.0 + (a - 1.0) * ka_ref[...])

    row = lax.broadcasted_iota(jnp.int32, (C, C), 0)
    col = lax.broadcasted_iota(jnp.int32, (C, C), 1)
    incl = col <= row
    strict = col < row
    cum = _dot(incl.astype(F32), lw, precision=lax.Precision.HIGHEST)
    dec = jnp.exp(cum)
    dec_prev = jnp.exp(cum - lw)
    dec_inv = jnp.exp(-cum)

    eye = (col == row).astype(F32)
    mask2 = jnp.concatenate([strict, incl], axis=0)

    for h0 in range(0, n_heads, head_group):
        hs = range(h0, min(h0 + head_group, n_heads))
        sls = {h: slice(h * head, (h + 1) * head) for h in hs}
        ar, bk, vb, s_old = {}, {}, {}, {}
        for h in hs:
            sl = sls[h]
            kk = kk_raw[:, sl]
            kk = kk / jnp.maximum(jnp.sqrt(jnp.sum(kk * kk, axis=-1, keepdims=True)), 1e-12)
            a_t = (-kk) * dec_prev[:, sl]
            r_t = r[:, sl] * dec[:, sl]
            b_t = (kk * a[:, sl]) * dec_inv[:, sl]
            k_t = k[:, sl] * dec_inv[:, sl]
            ar[h] = jnp.concatenate([a_t, r_t], axis=0).astype(BF16)
            bk[h] = (b_t.astype(BF16), k_t.astype(BF16))
            vb[h] = v[:, sl].astype(BF16)
            s_old[h] = s_scr[h]
        g_b = {h: jnp.where(mask2, _dot_nt(ar[h], bk[h][0]), 0.0) for h in hs}
        g_k = {h: jnp.where(mask2, _dot_nt(ar[h], bk[h][1]), 0.0).astype(BF16) for h in hs}
        pw_l = {h: g_b[h][:C] for h in hs}
        inv = {h: eye + pw_l[h] for h in hs}
        n = 1
        while 2 * n < C:
            pw_b = {h: pw_l[h].astype(BF16) for h in hs}
            pw_l = {h: _dot(pw_b[h], pw_b[h]) for h in hs}
            inv = {h: inv[h] + _dot(inv[h].astype(BF16), pw_l[h].astype(BF16)) for h in hs}
            n *= 2
        xy = {h: _dot_nt(ar[h], s_old[h].astype(BF16)) + _dot(g_k[h], vb[h]) for h in hs}
        u_b = {h: _dot(inv[h].astype(BF16), xy[h][:C].astype(BF16)).astype(BF16) for h in hs}
        y = {h: xy[h][C:] + _dot(g_b[h][C:].astype(BF16), u_b[h]) for h in hs}
        for h in hs:
            sl = sls[h]
            upd = _dot_tn(jnp.concatenate([u_b[h], vb[h]], axis=0),
                          jnp.concatenate(bk[h], axis=0))
            s_scr[h] = (s_old[h] + upd) * dec[C - 1:C, sl]
            mean = jnp.mean(y[h], axis=-1, keepdims=True)
            yc = y[h] - mean
            var = jnp.mean(yc * yc, axis=-1, keepdims=True)
            yn = yc * lax.rsqrt(var + GN_EPS) * lnw_ref[:, sl] + lnb_ref[:, sl]
            bonus = jnp.sum(r[:, sl] * k[:, sl] * rk_ref[:, sl], axis=-1, keepdims=True) * v[:, sl]
            y_scr[:, sl] = yn + bonus

    o_ref[...] = (y_scr[...] * g).astype(o_ref.dtype)

    @pl.when(c_idx == pl.num_programs(1) - 1)
    def _():
        slast_ref[0] = s_scr[...]


def _rwkv_mixer(p, row0, bsz, t, shift0, s0, vfirst, prm, chunk):
    n_heads, head = s0.shape[1], s0.shape[2]
    rd = n_heads * head
    rw = prm["mu"].shape[1]
    has_vres = vfirst is not None
    nc = t // chunk
    blk0 = row0 // chunk
    np_ = p.shape[1]

    def full(a):
        return pl.BlockSpec(a.shape, lambda b, c: (0,) * a.ndim)

    ins = [p, shift0, s0]
    specs = [pl.BlockSpec((chunk, np_), lambda b, c: (blk0 + b * nc + c, 0)),
             pl.BlockSpec((1, 1, rw), lambda b, c: (b, 0, 0)),
             pl.BlockSpec((1, n_heads, head, head), lambda b, c: (b, 0, 0, 0))]
    if has_vres:
        ins.append(vfirst)
        specs.append(pl.BlockSpec((chunk, rd), lambda b, c: (b * nc + c, 0)))
    names = ["mu", "w0", "wdec", "a0", "wiclr", "wgate", "k_k", "k_a", "r_k", "lnx_w", "lnx_b"]
    if has_vres:
        names += ["v0", "vra", "vrb"]
    for nm in names:
        ins.append(prm[nm])
        specs.append(full(prm[nm]))

    row_spec = pl.BlockSpec((chunk, rd), lambda b, c: (b * nc + c, 0))
    outs = [jax.ShapeDtypeStruct((bsz * t, rd), BF16)]
    out_specs = [row_spec]
    if not has_vres:
        outs.append(jax.ShapeDtypeStruct((bsz * t, rd), F32))
        out_specs.append(row_spec)
    outs.append(jax.ShapeDtypeStruct(s0.shape, F32))
    out_specs.append(pl.BlockSpec((1, n_heads, head, head), lambda b, c: (b, 0, 0, 0)))

    res = pl.pallas_call(
        functools.partial(_rwkv_kernel, n_heads=n_heads, head=head, chunk=chunk, has_vres=has_vres,
                          head_group=n_heads),
        grid=(bsz, nc),
        in_specs=specs,
        out_specs=out_specs,
        out_shape=outs,
        scratch_shapes=[pltpu.VMEM((n_heads, head, head), F32),
                        pltpu.VMEM((1, rw), F32),
                        pltpu.VMEM((chunk, rd), F32)],
        compiler_params=_params("parallel", "arbitrary"),
        name="rwkv_mixer",
    )(*ins)
    if has_vres:
        o, s_last = res
        return o, vfirst, s_last
    o, v_out, s_last = res
    return o, v_out, s_last


def _rotate(x, cos, sin_a, sin_b):
    half = LANES // 4
    return x * cos + pltpu.roll(x, LANES - half, 1) * sin_a + pltpu.roll(x, half, 1) * sin_b


def _mla_proj_kernel(qa_ref, kv_ref, cos_ref, sina_ref, sinb_ref, gqa_ref, wqb_ref, gqn_ref, gqr_ref,
                     gkva_ref, gkr_ref, q_ref, rows_ref, c_ref, kr_ref, *, n_heads, rope, scale):
    cos, sin_a, sin_b = cos_ref[...], sina_ref[...], sinb_ref[...]
    hq = _rms(qa_ref[...], gqa_ref[...]).astype(BF16)
    q = _dot(hq, wqb_ref[...])
    hw = n_heads * LANES
    for h in range(n_heads):
        qn = _rms(q[:, h * LANES:(h + 1) * LANES], gqn_ref[...]) * scale
        qr = _rms(q[:, hw + h * LANES:hw + (h + 1) * LANES], gqr_ref[...], n=rope)
        qr = _rotate(qr, cos, sin_a, sin_b) * scale
        q_ref[:, 2 * h * LANES:(2 * h + 1) * LANES] = qn.astype(q_ref.dtype)
        q_ref[:, (2 * h + 1) * LANES:(2 * h + 2) * LANES] = qr.astype(q_ref.dtype)
    kv = kv_ref[...]
    lat = gkva_ref.shape[1]
    c = _rms(kv[:, :lat], gkva_ref[...])
    kr = _rotate(_rms(kv[:, lat:lat + LANES], gkr_ref[...], n=rope), cos, sin_a, sin_b)
    rows_ref[:, :lat] = c
    rows_ref[:, lat:] = kr[:, :rope]
    c_ref[...] = c.astype(c_ref.dtype)
    kr_ref[...] = kr.astype(kr_ref.dtype)


def _mla_proj(p, col_q, col_kv, tabs, prm, n_heads, rope, scale):
    m = p.shape[0]
    tm = _pick(m, (512, 256, 128, 64, 32, 16, 8))
    qw = prm["g_qa"].shape[1]
    lat = prm["g_kva"].shape[1]
    assert qw == 512 and col_q % qw == 0 and col_kv % qw == 0

    def full(a):
        return pl.BlockSpec(a.shape, lambda i: (0,) * a.ndim)

    tab_spec = pl.BlockSpec((tm, LANES), lambda i: (i, 0))
    ws = [prm["g_qa"], prm["w_qb"], prm["g_qn"], prm["g_qr"], prm["g_kva"], prm["g_kr"]]
    return pl.pallas_call(
        functools.partial(_mla_proj_kernel, n_heads=n_heads, rope=rope, scale=scale),
        grid=(m // tm,),
        in_specs=[pl.BlockSpec((tm, qw), lambda i: (i, col_q // qw)),
                  pl.BlockSpec((tm, qw), lambda i: (i, col_kv // qw)),
                  tab_spec, tab_spec, tab_spec] + [full(a) for a in ws],
        out_specs=[pl.BlockSpec((tm, 2 * n_heads * LANES), lambda i: (i, 0)),
                   pl.BlockSpec((tm, lat + rope), lambda i: (i, 0)),
                   pl.BlockSpec((tm, lat), lambda i: (i, 0)),
                   pl.BlockSpec((tm, LANES), lambda i: (i, 0))],
        out_shape=[jax.ShapeDtypeStruct((m, 2 * n_heads * LANES), BF16),
                   jax.ShapeDtypeStruct((m, lat + rope), F32),
                   jax.ShapeDtypeStruct((m, lat), BF16),
                   jax.ShapeDtypeStruct((m, LANES), BF16)],
        compiler_params=_params("parallel"),
        name="mla_proj",
    )(p, p, *tabs, *ws)


def _kv_expand_kernel(c_ref, kr_ref, wuk_ref, wuv_ref, gkn_ref, k_ref, v_ref, *, n_heads):
    c = c_ref[...]
    ke = _dot(c, wuk_ref[...])
    kr = kr_ref[...]
    for h in range(n_heads):
        kh = _rms(ke[:, h * LANES:(h + 1) * LANES], gkn_ref[...])
        k_ref[:, 2 * h * LANES:(2 * h + 1) * LANES] = kh.astype(k_ref.dtype)
        k_ref[:, (2 * h + 1) * LANES:(2 * h + 2) * LANES] = kr
    v_ref[...] = _dot_nt(wuv_ref[...], c).astype(v_ref.dtype)


def _kv_expand(c, kr, mp, wuk, wuv_t, gkn, n_heads):
    lat = c.shape[1]
    tm = _pick(mp, (512, 256, 128))
    assert wuk.shape[1] == n_heads * LANES and wuv_t.shape[0] == n_heads * LANES

    def full(a):
        return pl.BlockSpec(a.shape, lambda i: (0,) * a.ndim)

    return pl.pallas_call(
        functools.partial(_kv_expand_kernel, n_heads=n_heads),
        grid=(mp // tm,),
        in_specs=[pl.BlockSpec((tm, lat), lambda i: (i, 0)),
                  pl.BlockSpec((tm, LANES), lambda i: (i, 0)),
                  full(wuk), full(wuv_t), full(gkn)],
        out_specs=[pl.BlockSpec((tm, 2 * n_heads * LANES), lambda i: (i, 0)),
                   pl.BlockSpec((n_heads * LANES, tm), lambda i: (0, i))],
        out_shape=[jax.ShapeDtypeStruct((mp, 2 * n_heads * LANES), BF16),
                   jax.ShapeDtypeStruct((n_heads * LANES, mp), BF16)],
        compiler_params=_params("parallel"),
        name="kv_expand",
    )(c, kr, wuk, wuv_t, gkn)


def _flash_kernel(qi_ref, ki_ref, q_ref, k_ref, v_ref, o_ref, m_scr, l_scr, acc_scr, *, tq, tk):
    pair = pl.program_id(2)
    qi, ki = qi_ref[pair], ki_ref[pair]

    @pl.when(ki == 0)
    def _():
        m_scr[...] = jnp.full(m_scr.shape, -jnp.inf, F32)
        l_scr[...] = jnp.zeros(l_scr.shape, F32)
        acc_scr[...] = jnp.zeros(acc_scr.shape, F32)

    s = _dot_nt(k_ref[...], q_ref[...])
    key = ki * tk + lax.broadcasted_iota(jnp.int32, s.shape, 0)
    qry = qi * tq + lax.broadcasted_iota(jnp.int32, s.shape, 1)
    s = jnp.where(key <= qry, s, -jnp.inf)
    m_old = m_scr[...]
    m_new = jnp.maximum(m_old, jnp.max(s, axis=0, keepdims=True))
    alpha = jnp.exp(m_old - m_new)
    p = jnp.exp(s - m_new)
    l_scr[...] = alpha * l_scr[...] + jnp.sum(p, axis=0, keepdims=True)
    acc_scr[...] = alpha * acc_scr[...] + _dot(v_ref[...], p.astype(BF16))
    m_scr[...] = m_new

    @pl.when((ki + 1) * tk >= (qi + 1) * tq)
    def _():
        o_ref[...] = (acc_scr[...] / l_scr[...]).T.astype(o_ref.dtype)


def _flash_prompt(q, k, v, bsz, t, n_heads):
    tk = _pick(t, (1024, 512, 256, 128))
    tq = tk if tk >= 1024 else (2 * tk if t % (2 * tk) == 0 else tk)
    nq, nk = t // tq, t // tk
    pairs = [(qi, ki) for qi in range(nq) for ki in range((qi + 1) * tq // tk)]
    qi_arr = jnp.asarray([pr[0] for pr in pairs], jnp.int32)
    ki_arr = jnp.asarray([pr[1] for pr in pairs], jnp.int32)
    grid_spec = pltpu.PrefetchScalarGridSpec(
        num_scalar_prefetch=2,
        grid=(bsz, n_heads, len(pairs)),
        in_specs=[pl.BlockSpec((tq, 2 * LANES), lambda b, h, pr, qa, ka: (b * nq + qa[pr], h)),
                  pl.BlockSpec((tk, 2 * LANES), lambda b, h, pr, qa, ka: (b * nk + ka[pr], h)),
                  pl.BlockSpec((LANES, tk), lambda b, h, pr, qa, ka: (h, b * nk + ka[pr]))],
        out_specs=pl.BlockSpec((tq, LANES), lambda b, h, pr, qa, ka: (b * nq + qa[pr], h)),
        scratch_shapes=[pltpu.VMEM((1, tq), F32), pltpu.VMEM((1, tq), F32), pltpu.VMEM((LANES, tq), F32)],
    )
    return pl.pallas_call(
        functools.partial(_flash_kernel, tq=tq, tk=tk),
        grid_spec=grid_spec,
        out_shape=jax.ShapeDtypeStruct((bsz * t, n_heads * LANES), BF16),
        compiler_params=_params("parallel", "parallel", "arbitrary"),
        name="flash_prompt",
    )(qi_arr, ki_arr, q, k, v)


def _sample_attn_kernel(pt_ref, q_ref, cn_ref, krn_ref, wuk3_ref, wukt_ref, wuv3_ref, gkn_ref, *rest,
                        n_heads, n_tok, n_page_in, sub_pages, lat, rope):
    page_refs = rest[:n_page_in]
    o_ref = rest[n_page_in]
    lhs_scr, qr_scr, m_scr, l_scr, acc_scr, s_scr, c_scr = rest[n_page_in + 1:]
    j = pl.program_id(1)
    nq = n_heads * n_tok
    hk = n_heads * LANES

    @pl.when(j == 0)
    def _():
        lhs_scr[:hk, :] = wukt_ref[...]
        for h in range(n_heads):
            qn = (q_ref[:, 2 * h * LANES:(2 * h + 1) * LANES].astype(F32) * gkn_ref[...]).astype(BF16)
            lhs_scr[hk + h * n_tok:hk + (h + 1) * n_tok, :] = _dot_nt(qn, wuk3_ref[h]).astype(BF16)
            qr_scr[h * n_tok:(h + 1) * n_tok, :] = q_ref[:, (2 * h + 1) * LANES:(2 * h + 2) * LANES]
        m_scr[...] = jnp.full(m_scr.shape, -jnp.inf, F32)
        l_scr[...] = jnp.zeros(l_scr.shape, F32)
        acc_scr[...] = jnp.zeros(acc_scr.shape, F32)
        s_scr[...] = jnp.full(s_scr.shape, NEUTRAL_SCORE, F32)
        c_scr[...] = jnp.zeros(c_scr.shape, BF16)

    def scores(big, s_rope):
        npos = big.shape[1]
        invs = []
        for h in range(n_heads):
            kh = big[h * LANES:(h + 1) * LANES]
            ms = jnp.sum(kh * kh, axis=0, keepdims=True) * (1.0 / LANES)
            invs.append(jnp.broadcast_to(lax.rsqrt(ms + RMS_EPS), (n_tok, npos)))
        return big[hk:] * jnp.concatenate(invs, axis=0) + s_rope

    diag = (lax.broadcasted_iota(jnp.int32, (nq, nq), 0) == lax.broadcasted_iota(jnp.int32, (nq, nq), 1))

    def as_row(col):
        return jnp.sum(jnp.where(diag, col, 0.0), axis=0, keepdims=True)

    def load_tile(sub):
        pages = [page_refs[sub * sub_pages + i][0, 0] for i in range(sub_pages)]
        c_t = jnp.concatenate([pg[:lat].astype(BF16) for pg in pages], axis=1)
        kr_t = jnp.concatenate([pg[lat:lat + rope].astype(BF16) for pg in pages], axis=1)
        return c_t, _dot(lhs_scr[...], c_t), _dot(qr_scr[:, :rope], kr_t)

    def softmax_update():
        s = s_scr[...]
        m_old = m_scr[...]
        m_new = jnp.maximum(m_old, jnp.max(s, axis=-1, keepdims=True))
        alpha = jnp.exp(m_old - m_new)
        p = jnp.exp(s - m_new)
        l_scr[...] = alpha * l_scr[...] + jnp.sum(p, axis=-1, keepdims=True)
        acc_scr[...] = as_row(alpha) * acc_scr[...] + _dot_nt(c_scr[...], p.astype(BF16))
        m_scr[...] = m_new

    n_sub = n_page_in // sub_pages
    nxt = load_tile(0)
    softmax_update()
    for sub in range(n_sub):
        c_t, big, s_rope = nxt
        if sub + 1 < n_sub:
            nxt = load_tile(sub + 1)
        npos = c_t.shape[1]
        s_scr[:, sub * npos:(sub + 1) * npos] = scores(big, s_rope)
        c_scr[:, sub * npos:(sub + 1) * npos] = c_t

    @pl.when(j == pl.num_programs(1) - 1)
    def _():
        softmax_update()
        c_n = cn_ref[...]
        row = lax.broadcasted_iota(jnp.int32, (nq, n_tok), 0)
        col = lax.broadcasted_iota(jnp.int32, (nq, n_tok), 1)
        s = scores(_dot_nt(lhs_scr[...], c_n), _dot_nt(qr_scr[:, :rope], krn_ref[:, :rope]))
        s = jnp.where(col <= (row % n_tok), s, -jnp.inf)
        m_old = m_scr[...]
        m_new = jnp.maximum(m_old, jnp.max(s, axis=-1, keepdims=True))
        alpha = jnp.exp(m_old - m_new)
        p = jnp.exp(s - m_new)
        l_fin = alpha * l_scr[...] + jnp.sum(p, axis=-1, keepdims=True)
        acc_fin = (as_row(alpha) * acc_scr[...]).T + _dot(p.astype(BF16), c_n)
        o_lat = (acc_fin / l_fin).astype(BF16)
        for h in range(n_heads):
            o_h = _dot(o_lat[h * n_tok:(h + 1) * n_tok], wuv3_ref[h])
            o_ref[:, h * LANES:(h + 1) * LANES] = o_h.astype(o_ref.dtype)


def _sample_attn(q, c_new, kr_new, row0, cache_t, layer, page_table, wuk3, wukt, wuv3, gkn,
                 n_tok, n_heads, rope):
    bs, n_pages = page_table.shape
    lat = c_new.shape[1]
    kvw, page = cache_t.shape[2], cache_t.shape[3]
    n_page_in = _pick(n_pages, (16, 8, 4, 2, 1))
    sub_pages = _pick(n_page_in, (8, 4, 2, 1))
    steps = n_pages // n_page_in
    blk0 = row0 // n_tok
    nq = n_heads * n_tok

    def full(a):
        return pl.BlockSpec(a.shape, lambda b, j, pt: (0,) * a.ndim)

    def page_spec(i):
        return pl.BlockSpec((1, 1, kvw, page),
                            lambda b, j, pt: (pt[b * n_pages + j * n_page_in + i], layer, 0, 0))

    grid_spec = pltpu.PrefetchScalarGridSpec(
        num_scalar_prefetch=1,
        grid=(bs, steps),
        in_specs=[pl.BlockSpec((n_tok, 2 * n_heads * LANES), lambda b, j, pt: (blk0 + b, 0)),
                  pl.BlockSpec((n_tok, lat), lambda b, j, pt: (blk0 + b, 0)),
                  pl.BlockSpec((n_tok, LANES), lambda b, j, pt: (blk0 + b, 0)),
                  full(wuk3), full(wukt), full(wuv3), full(gkn)] + [page_spec(i) for i in range(n_page_in)],
        out_specs=pl.BlockSpec((n_tok, n_heads * LANES), lambda b, j, pt: (b, 0)),
        scratch_shapes=[pltpu.VMEM((n_heads * LANES + nq, lat), BF16), pltpu.VMEM((nq, LANES), BF16),
                        pltpu.VMEM((nq, 1), F32), pltpu.VMEM((nq, 1), F32), pltpu.VMEM((lat, nq), F32),
                        pltpu.VMEM((nq, n_page_in * page), F32), pltpu.VMEM((lat, n_page_in * page), BF16)],
    )
    return pl.pallas_call(
        functools.partial(_sample_attn_kernel, n_heads=n_heads, n_tok=n_tok, n_page_in=n_page_in,
                          sub_pages=sub_pages, lat=lat, rope=rope),
        grid_spec=grid_spec,
        out_shape=jax.ShapeDtypeStruct((bs * n_tok, n_heads * LANES), BF16),
        compiler_params=_params("parallel", "arbitrary"),
        name="sample_attn",
    )(page_table.reshape(-1), q, c_new, kr_new, wuk3, wukt, wuv3, gkn, *([cache_t] * n_page_in))


def _out_proj_kernel(x_ref, a_ref, b_ref, w_ref, o_ref):
    ka = a_ref.shape[1]
    o_ref[...] = x_ref[...] + _dot(a_ref[...], w_ref[:ka, :]) + _dot(b_ref[...], w_ref[ka:, :])


def _out_proj(x, a, b, w):
    m, d = x.shape
    tm = _pick(m, (1024, 512, 256, 128, 64, 32, 16, 8))
    tn = _pick(d, (1024, 512, 256, 128))
    return pl.pallas_call(
        _out_proj_kernel,
        grid=(m // tm, d // tn),
        in_specs=[pl.BlockSpec((tm, tn), lambda i, j: (i, j)),
                  pl.BlockSpec((tm, a.shape[1]), lambda i, j: (i, 0)),
                  pl.BlockSpec((tm, b.shape[1]), lambda i, j: (i, 0)),
                  pl.BlockSpec((w.shape[0], tn), lambda i, j: (0, j))],
        out_specs=pl.BlockSpec((tm, tn), lambda i, j: (i, j)),
        out_shape=jax.ShapeDtypeStruct((m, d), F32),
        compiler_params=_params("parallel", "parallel"),
        name="out_proj",
    )(x, a, b, w)


def _ffn_kernel(x_ref, g_ref, wg_ref, wu_ref, wd_ref, o_ref, hn_ref):
    @pl.when(pl.program_id(1) == 0)
    def _():
        x = x_ref[...]
        hn_ref[...] = _rms(x, g_ref[...]).astype(BF16)
        o_ref[...] = x

    h = hn_ref[...]
    gt = _dot(h, wg_ref[...])
    act = (gt * jax.nn.sigmoid(gt) * _dot(h, wu_ref[...])).astype(BF16)
    o_ref[...] += _dot(act, wd_ref[...])


def _ffn_dense(x, g, wg, wu, wd):
    m, d = x.shape
    f = wg.shape[1]
    tm = _pick(m, (1024, 512, 256, 128, 64, 32, 16, 8))
    tf = _pick(f, (512, 256, 128))
    return pl.pallas_call(
        _ffn_kernel,
        grid=(m // tm, f // tf),
        in_specs=[pl.BlockSpec((tm, d), lambda i, j: (i, 0)),
                  pl.BlockSpec((1, d), lambda i, j: (0, 0)),
                  pl.BlockSpec((d, tf), lambda i, j: (0, j)),
                  pl.BlockSpec((d, tf), lambda i, j: (0, j)),
                  pl.BlockSpec((tf, d), lambda i, j: (j, 0))],
        out_specs=pl.BlockSpec((tm, d), lambda i, j: (i, 0)),
        out_shape=jax.ShapeDtypeStruct((m, d), F32),
        scratch_shapes=[pltpu.VMEM((tm, d), BF16)],
        compiler_params=_params("parallel", "arbitrary"),
        name="ffn_dense",
    )(x, g, wg, wu, wd)


def _router_kernel(x_ref, g_ref, wr_ref, hn_ref, lg_ref):
    hn = _rms(x_ref[...], g_ref[...])
    hn_ref[...] = hn.astype(hn_ref.dtype)
    lg_ref[...] = _dot(hn, wr_ref[...], precision=lax.Precision.HIGHEST)


def _router(x, g, wr):
    m, d = x.shape
    tm = _pick(m, (512, 256, 128, 64, 32, 16, 8))
    return pl.pallas_call(
        _router_kernel,
        grid=(m // tm,),
        in_specs=[pl.BlockSpec((tm, d), lambda i: (i, 0)),
                  pl.BlockSpec((1, d), lambda i: (0, 0)),
                  pl.BlockSpec(wr.shape, lambda i: (0, 0))],
        out_specs=[pl.BlockSpec((tm, d), lambda i: (i, 0)),
                   pl.BlockSpec((tm, wr.shape[1]), lambda i: (i, 0))],
        out_shape=[jax.ShapeDtypeStruct((m, d), F32),
                   jax.ShapeDtypeStruct((m, wr.shape[1]), F32)],
        compiler_params=_params("parallel"),
        name="router",
    )(x, g, wr)


def _experts_kernel(blk_e_ref, n_used_ref, row_tok_ref, hn_hbm, gate_ref, wg_ref, wu_ref, wd_ref, o_ref,
                    rows_buf, xs_scr, sems, *, bm):
    i, j = pl.program_id(0), pl.program_id(1)
    n_used = n_used_ref[0]

    def block_copy(slot):
        return pltpu.make_async_copy(hn_hbm.at[pl.ds(0, bm)], rows_buf.at[slot], sems.at[slot])

    def start_gather(block, slot):
        def body(r, carry):
            tok = row_tok_ref[block * bm + r]
            pltpu.make_async_copy(hn_hbm.at[pl.ds(tok, 1)], rows_buf.at[slot, pl.ds(r, 1)],
                                  sems.at[slot]).start()
            return carry
        lax.fori_loop(0, bm, body, 0, unroll=8)

    @pl.when(j == 0)
    def _():
        o_ref[...] = jnp.zeros(o_ref.shape, o_ref.dtype)

        @pl.when(i == 0)
        def _():
            start_gather(0, 0)

        @pl.when(i < n_used)
        def _():
            slot = i % 2
            block_copy(slot).wait()
            xs_scr[...] = rows_buf[slot].astype(BF16)

        @pl.when(i + 1 < n_used)
        def _():
            start_gather(i + 1, (i + 1) % 2)

    @pl.when(i < n_used)
    def _():
        x = xs_scr[...]
        gt = _dot(x, wg_ref[0].astype(BF16))
        act = (gt * jax.nn.sigmoid(gt) * _dot(x, wu_ref[0].astype(BF16))).astype(BF16)
        o_ref[...] += _dot(act, wd_ref[0].astype(BF16))

    @pl.when(j == pl.num_programs(1) - 1)
    def _():
        o_ref[...] = o_ref[...] * gate_ref[...]


def _experts(hn, row_tok, row_gate, blk_e, n_used, wg, wu, wd, bm):
    n_rows = row_tok.shape[0]
    d = hn.shape[1]
    assert hn.shape[0] >= bm
    f = wg.shape[2]
    tf = _pick(f, (256, 128))
    grid_spec = pltpu.PrefetchScalarGridSpec(
        num_scalar_prefetch=3,
        grid=(n_rows // bm, f // tf),
        in_specs=[pl.BlockSpec(memory_space=pl.ANY),
                  pl.BlockSpec((bm, 1), lambda i, j, be, nu, rt: (i, 0)),
                  pl.BlockSpec((1, d, tf), lambda i, j, be, nu, rt: (be[i], 0, j)),
                  pl.BlockSpec((1, d, tf), lambda i, j, be, nu, rt: (be[i], 0, j)),
                  pl.BlockSpec((1, tf, d), lambda i, j, be, nu, rt: (be[i], j, 0))],
        out_specs=pl.BlockSpec((bm, d), lambda i, j, be, nu, rt: (i, 0)),
        scratch_shapes=[pltpu.VMEM((2, bm, d), F32), pltpu.VMEM((bm, d), BF16),
                        pltpu.SemaphoreType.DMA((2,))],
    )
    return pl.pallas_call(
        functools.partial(_experts_kernel, bm=bm),
        grid_spec=grid_spec,
        out_shape=jax.ShapeDtypeStruct((n_rows, d), F32),
        compiler_params=_params("arbitrary", "arbitrary"),
        name="experts",
    )(blk_e, n_used, row_tok, hn, row_gate, wg, wu, wd)


def _moe(x, g, wr, wg, wu, wd, splits):
    m, d = x.shape
    n_exp = wg.shape[0]
    wr_pad = jnp.pad(wr, ((0, 0), (0, LANES - n_exp)))
    hn, logits = _router(x, g, wr_pad)
    top_logit, top_e = lax.top_k(logits[:, :n_exp], TOP_K)
    gates = jax.nn.softmax(top_logit, axis=-1)
    n_assign = m * TOP_K
    bm = _pick(m, (1024, 512, 256, 128, 64, 32, 16, 8))
    flat_e = top_e.reshape(-1)
    onehot = (flat_e[:, None] == jnp.arange(n_exp, dtype=flat_e.dtype)[None, :]).astype(jnp.int32)
    csum = jnp.cumsum(onehot, axis=0)
    counts = csum[-1]
    rank = jnp.sum((csum - onehot) * onehot, axis=1)
    padded = (counts + bm - 1) // bm * bm
    pad_end = jnp.cumsum(padded)
    pad_start = pad_end - padded
    slot = pad_start[flat_e] + rank
    n_rows = _round_up(n_assign + n_exp * (bm - 1), bm)
    n_blocks = n_rows // bm
    flat_tok = jnp.repeat(jnp.arange(m, dtype=jnp.int32), TOP_K)
    assert m < 2 ** 24
    fill = jnp.zeros((n_rows, 2), F32)
    row_info = fill.at[slot].set(jnp.stack([flat_tok.astype(F32), gates.reshape(-1)], axis=1))
    row_tok = row_info[:, 0].astype(jnp.int32)
    row_gate = row_info[:, 1]
    blk_start = jnp.arange(n_blocks, dtype=jnp.int32) * bm
    blk_e = jnp.minimum(jnp.sum(pad_end[None, :] <= blk_start[:, None], axis=1), n_exp - 1).astype(jnp.int32)
    n_used = (pad_end[-1] // bm).astype(jnp.int32).reshape(1)
    yb = _experts(hn, row_tok, row_gate[:, None], blk_e, n_used, wg, wu, wd, bm)
    slot2 = slot.reshape(m, TOP_K)
    return [x[a:b] + yb[slot2[a:b, 0]] + yb[slot2[a:b, 1]] for a, b in splits]


def _pad_cols(a, width):
    return jnp.pad(a, [(0, 0)] * (a.ndim - 1) + [(0, width - a.shape[-1])])


def _rwkv_cols(a, rd, wl, al):
    o = 3 * rd
    return jnp.concatenate([a[..., :o], _pad_cols(a[..., o:o + wl], LANES),
                            _pad_cols(a[..., o + wl:o + wl + al], LANES),
                            _pad_cols(a[..., o + wl + al:], 2 * LANES)], axis=-1)


def _rwkv_cols_inv(a, rd, wl, al, gl):
    o = 3 * rd
    return jnp.concatenate([a[..., :o], a[..., o:o + wl], a[..., o + LANES:o + LANES + al],
                            a[..., o + 2 * LANES:o + 2 * LANES + gl]], axis=-1)


def _pad_rows(a, rows):
    return jnp.pad(a, ((0, rows - a.shape[0]), (0, 0)))


def _rope_tables(pos, rope):
    half = rope // 2
    assert half == LANES // 4
    inv_freq = ROPE_THETA ** (-jnp.arange(half, dtype=F32) * (2.0 / rope))
    ang = pos.astype(F32)[:, None] * inv_freq[None, :]
    cos, sin = jnp.cos(ang), jnp.sin(ang)
    z = jnp.zeros_like(cos)
    return (jnp.concatenate([cos, cos, z, z], axis=1),
            jnp.concatenate([-sin, z, z, z], axis=1),
            jnp.concatenate([z, sin, z, z], axis=1))


def kernel(x_prompt, x_sample, cache_kv, page_table, state_wkv, state_shift, g_mix, w_in, mu_shift, w0, w_decay, a0, w_iclr, w_gate_out, k_k, k_a, r_k, lnx_w, lnx_b, v0, v_res_a, v_res_b, g_qa, w_qb, g_qn, g_qr, g_kva, g_kr, w_uk, w_uv, g_kn, w_out, g_ffn, w_ffn_gate, w_ffn_up, w_ffn_down, w_router, w_exp_gate, w_exp_up, w_exp_down):
    bp, tp, d = x_prompt.shape
    bs, ts, _ = x_sample.shape
    depth = g_mix.shape[0]
    n_heads_r, head_r = r_k.shape[1], r_k.shape[2]
    rd = n_heads_r * head_r
    wl, al, gl = w_decay.shape[1], w_iclr.shape[1], w_gate_out.shape[1]
    rw_orig = 3 * rd + wl + al + gl
    ql, lat, rope, nope = g_qa.shape[1], g_kva.shape[1], g_kr.shape[1], g_qn.shape[1]
    n_heads_a, mla_v = w_uk.shape[2], w_uv.shape[3]
    assert nope == LANES and mla_v == LANES and rope == LANES // 2
    assert wl <= LANES and al <= LANES and gl <= 2 * LANES and v_res_a.shape[2] <= LANES
    past_len = page_table.shape[1] * PAGE_SIZE
    mp, ms = bp * tp, bs * ts
    scale = (nope + rope) ** -0.5
    rw = 3 * rd + 4 * LANES
    col_q, col_kv = rw, rw + ql
    chunk_p = _pick(tp, (64, 32, 16, 8))
    chunk_s = _pick(ts, (64, 32, 16, 8))

    x = jnp.concatenate([x_prompt.reshape(mp, d), x_sample.reshape(ms, d)], axis=0)
    pos = jnp.concatenate([jnp.tile(jnp.arange(tp, dtype=jnp.int32), bp),
                           jnp.tile(past_len + jnp.arange(ts, dtype=jnp.int32), bs)])
    tabs = _rope_tables(pos, rope)
    row2 = lambda a: a.reshape(1, -1)
    cache_t = jnp.transpose(cache_kv, (0, 1, 3, 2))

    shift0_p = jnp.zeros((bp, 1, rw), F32)
    wkv0_p = jnp.zeros((bp, n_heads_r, head_r, head_r), F32)
    vf_p = vf_s = None
    rows_p, rows_s, wkv_p, wkv_s, sh_p, sh_s = [], [], [], [], [], []
    for l in range(depth):
        w_in_l = w_in[l]
        w_pad = jnp.concatenate([
            _rwkv_cols(w_in_l[:, :rw_orig], rd, wl, al),
            w_in_l[:, rw_orig:rw_orig + ql + lat],
            _pad_cols(w_in_l[:, rw_orig + ql + lat:], 2 * LANES)], axis=1).astype(BF16)
        p = _in_proj(x, row2(g_mix[l]), w_pad)

        prm = {
            "mu": _rwkv_cols(row2(mu_shift[l]), rd, wl, al),
            "w0": row2(w0[l]), "a0": row2(a0[l]),
            "wdec": _pad_rows(w_decay[l], LANES).astype(BF16),
            "wiclr": _pad_rows(w_iclr[l], LANES).astype(BF16),
            "wgate": _pad_rows(w_gate_out[l], 2 * LANES).astype(BF16),
            "k_k": row2(k_k[l]), "k_a": row2(k_a[l]), "r_k": row2(r_k[l]),
            "lnx_w": row2(lnx_w[l]), "lnx_b": row2(lnx_b[l]),
        }
        if l > 0:
            prm["v0"] = row2(v0[l - 1])
            prm["vra"] = _pad_cols(v_res_a[l - 1], LANES).astype(BF16)
            prm["vrb"] = _pad_rows(v_res_b[l - 1], LANES).astype(BF16)
        oa_p, vf_p, wp = _rwkv_mixer(p, 0, bp, tp, shift0_p, wkv0_p, vf_p, prm, chunk_p)
        shift0_s = _rwkv_cols(state_shift[l], rd, wl, al)[:, None, :]
        oa_s, vf_s, ws = _rwkv_mixer(p, mp, bs, ts, shift0_s, state_wkv[l], vf_s, prm, chunk_s)

        wqb3 = w_qb[l].reshape(ql, n_heads_a, nope + rope)
        wqb_pad = jnp.concatenate([wqb3[:, :, :nope].reshape(ql, n_heads_a * nope),
                                   _pad_cols(wqb3[:, :, nope:], LANES).reshape(ql, n_heads_a * LANES)],
                                  axis=1).astype(BF16)
        mprm = {"g_qa": row2(g_qa[l]), "w_qb": wqb_pad, "g_qn": row2(g_qn[l]),
                "g_qr": _pad_cols(row2(g_qr[l]), LANES), "g_kva": row2(g_kva[l]),
                "g_kr": _pad_cols(row2(g_kr[l]), LANES)}
        q_cat, rows, c_bf, kr_bf = _mla_proj(p, col_q, col_kv, tabs, mprm, n_heads_a, rope, scale)

        wuk2 = w_uk[l].reshape(lat, n_heads_a * nope).astype(BF16)
        wuv_t = jnp.transpose(w_uv[l], (1, 2, 0)).reshape(n_heads_a * mla_v, lat).astype(BF16)
        gkn = row2(g_kn[l])
        k_cat, v_exp = _kv_expand(c_bf, kr_bf, mp, wuk2, wuv_t, gkn, n_heads_a)
        ob_p = _flash_prompt(q_cat, k_cat, v_exp, bp, tp, n_heads_a)

        wuk3 = jnp.transpose(w_uk[l], (1, 0, 2)).astype(BF16)
        wukt = jnp.transpose(w_uk[l], (1, 2, 0)).reshape(n_heads_a * nope, lat).astype(BF16)
        wuv3 = jnp.transpose(w_uv[l], (1, 0, 2)).astype(BF16)
        ob_s = _sample_attn(q_cat, c_bf, kr_bf, mp, cache_t, l, page_table, wuk3, wukt, wuv3, gkn,
                            ts, n_heads_a, rope)

        o_a = jnp.concatenate([oa_p, oa_s], axis=0)
        o_b = jnp.concatenate([ob_p, ob_s], axis=0)
        x = _out_proj(x, o_a, o_b, w_out[l].astype(BF16))

        i = l // 2
        if l % 2 == 0:
            x = _ffn_dense(x, row2(g_ffn[l]), w_ffn_gate[i].astype(BF16), w_ffn_up[i].astype(BF16),
                           w_ffn_down[i].astype(BF16))
        else:
            last = l == depth - 1
            parts = _moe(x, row2(g_ffn[l]), w_router[i], w_exp_gate[i], w_exp_up[i], w_exp_down[i],
                         [(0, mp), (mp, mp + ms)] if last else [(0, mp + ms)])
            x = None if last else parts[0]

        rows_p.append(rows[:mp].reshape(bp, tp, lat + rope))
        rows_s.append(rows[mp:].reshape(bs, ts, lat + rope))
        wkv_p.append(wp)
        wkv_s.append(ws)
        last_p = jnp.concatenate([p[b * tp + tp - 1:b * tp + tp, :rw] for b in range(bp)], axis=0)
        last_s = p[mp:].reshape(bs, ts, -1)[:, -1, :rw]
        sh_p.append(_rwkv_cols_inv(last_p, rd, wl, al, gl))
        sh_s.append(_rwkv_cols_inv(last_s, rd, wl, al, gl))

    if x is not None:
        parts = [x[:mp], x[mp:]]
    return (parts[0].reshape(bp, tp, d), parts[1].reshape(bs, ts, d),
            jnp.stack(rows_p), jnp.stack(rows_s), jnp.stack(wkv_p), jnp.stack(wkv_s),
            jnp.stack(sh_p), jnp.stack(sh_s))
```
